```python
import math
import jax
import jax.numpy as jnp
from jax import lax
import numpy as np

D_MODEL = 2048
BATCH = 8
SEQ = 2048
DEPTH = 2

MEM_LEN = 256
DA_QK_DIM = 64
DA_V_DIM = 2 * DA_QK_DIM
DA_HEADS = D_MODEL // (4 * DA_V_DIM)
DA_QK_WIDTH = DA_HEADS * 2 * DA_QK_DIM
DA_WIDTH = DA_HEADS * DA_V_DIM
Q_BLOCK = 128
ROPE_THETA = 10000.0
ML_DIM = 128
ML_HEADS = D_MODEL // (4 * ML_DIM)
ML_WIDTH = ML_HEADS * ML_DIM
ML_CHUNK = 64
GD_DIM = 128
GD_HEADS = D_MODEL // (2 * GD_DIM)
GD_WIDTH = GD_HEADS * GD_DIM
GD_CHUNK = 64
CONV_W = 4
MIX_WIDTH = DA_WIDTH + ML_WIDTH + GD_WIDTH
IN_SIZES = (DA_QK_WIDTH, DA_QK_WIDTH, DA_WIDTH,
            ML_WIDTH, ML_WIDTH, ML_WIDTH, ML_WIDTH, ML_HEADS, ML_HEADS,
            3 * GD_WIDTH, GD_WIDTH, GD_HEADS, GD_HEADS)
IN_WIDTH = sum(IN_SIZES)
XA_HEADS = 4
XA_DIM = D_MODEL // XA_HEADS
N_EXPERTS = 16
N_GROUPS = 4
GROUP_SIZE = N_EXPERTS // N_GROUPS
TOP_GROUPS = 1
TOP_K = 2
D_FF = D_MODEL // 2

kernel_name = 'hybrid_diffattn_mlstm_gdn_moe'

F32 = jnp.float32


def layer_norm(x, g, b, eps=1e-5):
    xf = x.astype(F32)
    mu = jnp.mean(xf, -1, keepdims=True)
    var = jnp.mean(jnp.square(xf - mu), -1, keepdims=True)
    return ((xf - mu) * lax.rsqrt(var + eps) * g + b).astype(x.dtype)


def rms_norm(x, g, eps=1e-6):
    xf = x.astype(F32)
    return (xf * lax.rsqrt(jnp.mean(xf * xf, -1, keepdims=True) + eps) * g).astype(x.dtype)


def l2_norm(x, eps=1e-6):
    return x * lax.rsqrt(jnp.sum(x * x, -1, keepdims=True) + eps)


def rotary_tables(T):
    inv = 1.0 / (ROPE_THETA ** (jnp.arange(0, DA_QK_DIM, 2, dtype=F32) / DA_QK_DIM))
    ang = jnp.arange(T, dtype=F32)[:, None] * inv[None, :]
    return jnp.cos(ang), jnp.sin(ang)


def apply_rotary(t, cos, sin):
    t1, t2 = jnp.split(t, 2, axis=-1)
    return jnp.concatenate([t1 * cos - t2 * sin, t1 * sin + t2 * cos], -1).astype(t.dtype)


def split_columns(proj):
    offs, acc = [], 0
    for s in IN_SIZES[:-1]:
        acc += s
        offs.append(acc)
    return jnp.split(proj, offs, axis=-1)


def to_chunks(t, L):
    B, T, H, D = t.shape
    return t.reshape(B, T // L, L, H, D).transpose(1, 0, 3, 2, 4)


def gate_chunks(g, L):
    B, T, H = g.shape
    return g.reshape(B, T // L, L, H).transpose(1, 0, 3, 2)


def from_chunks(t):
    NC, B, H, L, D = t.shape
    return t.transpose(1, 0, 3, 2, 4).reshape(B, NC * L, H, D)


def diff_attention(q, k, v, lam, norm_g, lambda_init, cos, sin):
    B, T, _ = q.shape
    q = q.reshape(B, T, DA_HEADS, 2, DA_QK_DIM).transpose(0, 2, 3, 1, 4)
    k = k.reshape(B, T, DA_HEADS, 2, DA_QK_DIM).transpose(0, 2, 3, 1, 4)
    v = v.reshape(B, T, DA_HEADS, DA_V_DIM).transpose(0, 2, 1, 3)
    q = apply_rotary(q, cos, sin) * (DA_QK_DIM ** -0.5)
    k = apply_rotary(k, cos, sin)
    lamf = lam.astype(F32)
    lam_full = (jnp.exp(jnp.sum(lamf[0] * lamf[1])) - jnp.exp(jnp.sum(lamf[2] * lamf[3]))
                + lambda_init)
    outs = []
    for qs in range(0, T, Q_BLOCK):
        ke = qs + Q_BLOCK
        s = jnp.einsum('bhcqd,bhckd->bhcqk', q[:, :, :, qs:ke], k[:, :, :, :ke]).astype(F32)
        causal = jnp.arange(qs, ke)[:, None] >= jnp.arange(ke)[None, :]
        p = jax.nn.softmax(jnp.where(causal, s, -jnp.inf), axis=-1)
        a = (p[:, :, 0] - lam_full * p[:, :, 1]).astype(v.dtype)
        outs.append(jnp.einsum('bhqk,bhkd->bhqd', a, v[:, :, :ke]))
    o = jnp.concatenate(outs, axis=2)
    o = rms_norm(o, norm_g) * (1.0 - lambda_init)
    return o.transpose(0, 2, 1, 3).reshape(B, T, DA_WIDTH)


def mlstm(q, k, v, o_pre, i_pre, f_pre, gate_b, norm_g):
    B, T, _ = q.shape
    H, D, L = ML_HEADS, ML_DIM, ML_CHUNK
    dtype = q.dtype
    qc = to_chunks(q.reshape(B, T, H, D).astype(F32), L)
    kc = to_chunks(k.reshape(B, T, H, D).astype(F32), L) * (D ** -0.5)
    vc = to_chunks(v.reshape(B, T, H, D).astype(F32), L)
    li = gate_chunks(i_pre.astype(F32) + gate_b[0], L)
    lf = gate_chunks(jax.nn.log_sigmoid(f_pre.astype(F32) + gate_b[1]), L)
    causal = jnp.tril(jnp.ones((L, L), bool))

    def step(carry, inp):
        C, n, m = carry
        qb, kb, vb, lib, lfb = inp
        b = jnp.cumsum(lfb, -1)
        logw = jnp.where(causal, b[..., :, None] - b[..., None, :] + lib[..., None, :], -jnp.inf)
        logp = b + m[..., None]
        m_row = jnp.maximum(logp, jnp.max(logw, -1))
        w = jnp.exp(logw - m_row[..., None])
        sp = jnp.exp(logp - m_row)
        qk = jnp.einsum('bhsd,bhjd->bhsj', qb, kb) * w
        num = (jnp.einsum('bhsj,bhje->bhse', qk, vb)
               + sp[..., None] * jnp.einsum('bhed,bhsd->bhse', C, qb))
        den = jnp.sum(qk, -1) + sp * jnp.einsum('bhd,bhsd->bhs', n, qb)
        h = num / jnp.maximum(jnp.abs(den), jnp.exp(-m_row))[..., None]
        bL = b[..., -1]
        logu = bL[..., None] - b + lib
        m_new = jnp.maximum(bL + m, jnp.max(logu, -1))
        u = jnp.exp(logu - m_new[..., None])
        decay = jnp.exp(bL + m - m_new)
        C = decay[..., None, None] * C + jnp.einsum('bhj,bhje,bhjd->bhed', u, vb, kb)
        n = decay[..., None] * n + jnp.einsum('bhj,bhjd->bhd', u, kb)
        return (C, n, m_new), h

    init = (jnp.zeros((B, H, D, D), F32), jnp.zeros((B, H, D), F32), jnp.zeros((B, H), F32))
    _, h = lax.scan(step, init, (qc, kc, vc, li, lf))
    h = rms_norm(from_chunks(h), norm_g)
    o = jax.nn.sigmoid(o_pre.astype(F32)).reshape(B, T, H, D)
    return (o * h).reshape(B, T, ML_WIDTH).astype(dtype)


def causal_conv_silu(x, w):
    C = x.shape[-1]
    y = lax.conv_general_dilated(x, w[:, None, :].astype(x.dtype), window_strides=(1,),
                                 padding=((CONV_W - 1, 0),),
                                 dimension_numbers=('NWC', 'WIO', 'NWC'),
                                 feature_group_count=C)
    return jax.nn.silu(y)


def gated_deltanet(qkv, z, a_pre, b_pre, conv_w, a_log, dt_bias, norm_g):
    B, T, _ = qkv.shape
    H, D, L = GD_HEADS, GD_DIM, GD_CHUNK
    dtype = qkv.dtype
    q, k, v = jnp.split(causal_conv_silu(qkv, conv_w).astype(F32), 3, axis=-1)
    q = to_chunks(l2_norm(q.reshape(B, T, H, D)) * (D ** -0.5), L)
    k = to_chunks(l2_norm(k.reshape(B, T, H, D)), L)
    v = to_chunks(v.reshape(B, T, H, D), L)
    beta = gate_chunks(jax.nn.sigmoid(b_pre.astype(F32)), L)
    g = gate_chunks(-jnp.exp(a_log.astype(F32)) * jax.nn.softplus(a_pre.astype(F32) + dt_bias), L)
    gc = jnp.cumsum(g, -1)
    tri = jnp.tril(jnp.ones((L, L), bool))
    strict = jnp.tril(jnp.ones((L, L), bool), -1)
    decay = jnp.exp(jnp.where(tri, gc[..., :, None] - gc[..., None, :], -jnp.inf))
    kb = k * beta[..., None]
    A = jnp.where(strict, jnp.einsum('nbhid,nbhjd->nbhij', kb, k) * decay, 0.0)
    rhs = jnp.concatenate([v * beta[..., None], kb * jnp.exp(gc)[..., None]], -1)
    uw = lax.linalg.triangular_solve(A + jnp.eye(L, dtype=F32), rhs, left_side=True, lower=True)
    u, w = uw[..., :D], uw[..., D:]
    attn = jnp.where(tri, jnp.einsum('nbhid,nbhjd->nbhij', q, k) * decay, 0.0)

    def step(S, inp):
        qi, ki, ui, wi, gi, ai = inp
        v_new = ui - jnp.einsum('bhld,bhde->bhle', wi, S)
        o = (jnp.einsum('bhld,bhde->bhle', qi * jnp.exp(gi)[..., None], S)
             + jnp.einsum('bhlj,bhje->bhle', ai, v_new))
        gl = gi[..., -1]
        S = (S * jnp.exp(gl)[..., None, None]
             + jnp.einsum('bhld,bhle->bhde', ki * jnp.exp(gl[..., None] - gi)[..., None], v_new))
        return S, o

    _, o = lax.scan(step, jnp.zeros((B, H, D, D), F32), (q, k, u, w, gc, attn))
    o = rms_norm(from_chunks(o), norm_g)
    zg = jax.nn.silu(z.astype(F32)).reshape(B, T, H, D)
    return (o * zg).reshape(B, T, GD_WIDTH).astype(dtype)


def hybrid_mixer(h, w_in, da_lambda, da_norm_g, ml_gate_b, ml_norm_g, gd_conv_w,
                 gd_a_log, gd_dt_bias, gd_norm_g, w_out, lambda_init, cos, sin):
    (da_q, da_k, da_v, ml_q, ml_k, ml_v, ml_o, ml_i, ml_f,
     gd_qkv, gd_z, gd_a, gd_b) = split_columns(h @ w_in)
    y_da = diff_attention(da_q, da_k, da_v, da_lambda, da_norm_g, lambda_init, cos, sin)
    y_ml = mlstm(ml_q, ml_k, ml_v, ml_o, ml_i, ml_f, ml_gate_b, ml_norm_g)
    y_gd = gated_deltanet(gd_qkv, gd_z, gd_a, gd_b, gd_conv_w, gd_a_log, gd_dt_bias, gd_norm_g)
    return jnp.concatenate([y_da, y_ml, y_gd], axis=-1) @ w_out


def memory_cross_attention(h, mem, wq, wkv, wo):
    B, T, _ = h.shape
    M = mem.shape[1]
    q = (h @ wq).reshape(B, T, XA_HEADS, XA_DIM)
    kv = (mem @ wkv).reshape(B, M, 2, XA_HEADS, XA_DIM)
    s = jnp.einsum('bthd,bmhd->bhtm', q, kv[:, :, 0]).astype(F32) * (XA_DIM ** -0.5)
    p = jax.nn.softmax(s, axis=-1).astype(h.dtype)
    o = jnp.einsum('bhtm,bmhd->bthd', p, kv[:, :, 1]).reshape(B, T, XA_HEADS * XA_DIM)
    return o @ wo


def grouped_moe(h, router_w, router_bias, w_in, w_out):
    B, T, D = h.shape
    t = h.reshape(B * T, D)
    scores = jax.nn.sigmoid((t @ router_w).astype(F32))
    sel = scores + router_bias.astype(F32)
    grp_score = jnp.sum(lax.top_k(sel.reshape(-1, N_GROUPS, GROUP_SIZE), 2)[0], -1)
    _, gi = lax.top_k(grp_score, TOP_GROUPS)
    gmask = jnp.any(gi[:, :, None] == jnp.arange(N_GROUPS)[None, None, :], axis=1)
    emask = jnp.repeat(gmask, GROUP_SIZE, axis=1)
    _, ei = lax.top_k(jnp.where(emask, sel, -jnp.inf), TOP_K)
    wsel = jnp.take_along_axis(scores, ei, axis=-1)
    wsel = wsel / jnp.sum(wsel, -1, keepdims=True)
    combine = jnp.sum(jax.nn.one_hot(ei, N_EXPERTS, dtype=F32) * wsel[..., None], axis=1)
    y = jnp.zeros_like(t)
    for e in range(N_EXPERTS):
        gate, up = jnp.split(t @ w_in[e], 2, axis=-1)
        y = y + combine[:, e:e + 1].astype(t.dtype) * ((jax.nn.silu(gate) * up) @ w_out[e])
    return y.reshape(B, T, D)


def setup_inputs(seed: int = 0) -> dict:
    key = jax.random.key(seed)
    ks = jax.random.split(key, 32)
    D = D_MODEL
    beta = (8 * DEPTH) ** -0.25

    def nrm(k, shape, s):
        return jax.random.normal(k, shape, F32) * s

    x = nrm(ks[0], (BATCH, SEQ, D), 1.0)
    mem = nrm(ks[1], (BATCH, MEM_LEN, D), 1.0)
    w_in = nrm(ks[2], (DEPTH, D, IN_WIDTH), D ** -0.5)
    da_lambda = nrm(ks[3], (DEPTH, 4, DA_QK_DIM), 0.1)
    da_norm_g = 1.0 + nrm(ks[4], (DEPTH, DA_V_DIM), 0.02)
    ml_gate_b = jnp.stack([nrm(ks[5], (DEPTH, ML_HEADS), 0.1),
                           3.0 + 3.0 * jax.random.uniform(ks[6], (DEPTH, ML_HEADS), F32)], axis=1)
    ml_norm_g = 1.0 + nrm(ks[7], (DEPTH, ML_DIM), 0.02)
    gd_conv_w = nrm(ks[8], (DEPTH, CONV_W, 3 * GD_WIDTH), CONV_W ** -0.5)
    gd_a_log = jnp.log(jax.random.uniform(ks[9], (DEPTH, GD_HEADS), F32, 1.0, 16.0))
    dt = jnp.exp(jax.random.uniform(ks[10], (DEPTH, GD_HEADS), F32,
                                    math.log(1e-3), math.log(1e-1)))
    gd_dt_bias = dt + jnp.log(-jnp.expm1(-dt))
    gd_norm_g = 1.0 + nrm(ks[11], (DEPTH, GD_DIM), 0.02)
    w_out = nrm(ks[12], (DEPTH, MIX_WIDTH, D), (MIX_WIDTH ** -0.5) * beta)
    xa_wq = nrm(ks[13], (DEPTH, D, XA_HEADS * XA_DIM), D ** -0.5)
    xa_wkv = nrm(ks[14], (DEPTH, D, 2 * XA_HEADS * XA_DIM), D ** -0.5)
    xa_wo = nrm(ks[15], (DEPTH, XA_HEADS * XA_DIM, D), ((XA_HEADS * XA_DIM) ** -0.5) * beta)
    router_w = nrm(ks[16], (D, N_EXPERTS), D ** -0.5)
    router_bias = nrm(ks[17], (N_EXPERTS,), 0.01)
    moe_w_in = nrm(ks[18], (DEPTH, N_EXPERTS, D, 2 * D_FF), D ** -0.5)
    moe_w_out = nrm(ks[19], (DEPTH, N_EXPERTS, D_FF, D), (D_FF ** -0.5) * beta)
    ln_g = 1.0 + nrm(ks[20], (DEPTH, 3, D), 0.02)
    ln_b = nrm(ks[21], (DEPTH, 3, D), 0.02)
    return {'x': x, 'mem': mem, 'w_in': w_in, 'da_lambda': da_lambda, 'da_norm_g': da_norm_g,
            'ml_gate_b': ml_gate_b, 'ml_norm_g': ml_norm_g, 'gd_conv_w': gd_conv_w,
            'gd_a_log': gd_a_log, 'gd_dt_bias': gd_dt_bias, 'gd_norm_g': gd_norm_g,
            'w_out': w_out, 'xa_wq': xa_wq, 'xa_wkv': xa_wkv, 'xa_wo': xa_wo,
            'router_w': router_w, 'router_bias': router_bias, 'moe_w_in': moe_w_in,
            'moe_w_out': moe_w_out, 'ln_g': ln_g, 'ln_b': ln_b}


def reference(x, mem, w_in, da_lambda, da_norm_g, ml_gate_b, ml_norm_g, gd_conv_w,
              gd_a_log, gd_dt_bias, gd_norm_g, w_out, xa_wq, xa_wkv, xa_wo,
              router_w, router_bias, moe_w_in, moe_w_out, ln_g, ln_b):
    T = x.shape[1]
    cos, sin = rotary_tables(T)
    alpha = (2 * DEPTH) ** 0.25
    h = x
    for l in range(DEPTH):
        lambda_init = 0.8 - 0.6 * math.exp(-0.3 * l)
        mix = hybrid_mixer(h, w_in[l], da_lambda[l], da_norm_g[l], ml_gate_b[l], ml_norm_g[l],
                           gd_conv_w[l], gd_a_log[l], gd_dt_bias[l], gd_norm_g[l], w_out[l],
                           lambda_init, cos, sin)
        h = layer_norm(alpha * h + mix, ln_g[l, 0], ln_b[l, 0])
        xa = memory_cross_attention(h, mem, xa_wq[l], xa_wkv[l], xa_wo[l])
        h = layer_norm(alpha * h + xa, ln_g[l, 1], ln_b[l, 1])
        ff = grouped_moe(h, router_w, router_bias, moe_w_in[l], moe_w_out[l])
        h = layer_norm(alpha * h + ff, ln_g[l, 2], ln_b[l, 2])
    return h
```

```python
import functools
import math

import jax
import jax.numpy as jnp
from jax import lax
from jax.experimental import pallas as pl
from jax.experimental.pallas import tpu as pltpu

F32 = jnp.float32
BF16 = jnp.bfloat16
I32 = jnp.int32

D_MODEL = 2048
HEAD = 128
DA_HEADS = 4
DA_QK_DIM = 64
ML_HEADS = 4
GD_HEADS = 8
CHUNK = 64
CONV_W = 4
ROPE_THETA = 10000.0
XA_HEADS = 4
XA_DIM = D_MODEL // XA_HEADS
N_EXPERTS = 16
N_GROUPS = 4
GROUP_SIZE = 4
D_FF = D_MODEL // 2
MAIN_WIDTH = 7680
GATE_ROWS = 32
NEG = -1e30
VMEM_LIMIT = 56 * 1024 * 1024
MOE_TILE = 256
ATT_BLOCK = 256

CB_DA_Q, CB_DA_K, CB_DA_V = 0, 4, 8
CB_ML_Q, CB_ML_K, CB_ML_V, CB_ML_O = 12, 16, 20, 24
CB_GD_Q, CB_GD_K, CB_GD_V, CB_GD_Z = 28, 36, 44, 52


def _params(*sem):
    return pltpu.CompilerParams(dimension_semantics=sem, vmem_limit_bytes=VMEM_LIMIT)


def _dot(a, b):
    return jnp.dot(a, b, preferred_element_type=F32)


def _dot_nt(a, b):
    return lax.dot_general(a, b, (((1,), (1,)), ((), ())), preferred_element_type=F32)


def _dot_tn(a, b):
    return lax.dot_general(a, b, (((0,), (0,)), ((), ())), preferred_element_type=F32)


def _sigmoid(x):
    return 1.0 / (1.0 + jnp.exp(-x))


def _col_from_row(row, n):
    ri = lax.broadcasted_iota(I32, (n, n), 0)
    ci = lax.broadcasted_iota(I32, (n, n), 1)
    return jnp.sum(jnp.where(ri == ci, jnp.broadcast_to(row, (n, n)), 0.0), axis=1, keepdims=True)


def _lane_cumsum(x, seg):
    lane = lax.broadcasted_iota(I32, x.shape, 1) % seg
    s = 1
    while s < seg:
        x = x + jnp.where(lane >= s, pltpu.roll(x, s, 1), 0.0)
        s *= 2
    return x


def _mm_kernel(x_ref, w_ref, o_ref, xb_ref):
    @pl.when(pl.program_id(1) == 0)
    def _():
        xb_ref[...] = x_ref[...].astype(BF16)

    o_ref[...] = _dot(xb_ref[...], w_ref[...]).astype(o_ref.dtype)


def _matmul(x, w, out_dtype, tm, tn):
    m, k = x.shape
    n = w.shape[1]
    tm = min(tm, m)
    return pl.pallas_call(
        _mm_kernel,
        grid=(m // tm, n // tn),
        in_specs=[pl.BlockSpec((tm, k), lambda i, j: (i, 0)),
                  pl.BlockSpec((k, tn), lambda i, j: (0, j))],
        out_specs=pl.BlockSpec((tm, tn), lambda i, j: (i, j)),
        out_shape=jax.ShapeDtypeStruct((m, n), out_dtype),
        scratch_shapes=[pltpu.VMEM((tm, k), BF16)],
        compiler_params=_params("arbitrary", "arbitrary"),
        name="matmul",
    )(x, w)


def _gate_kernel(g_ref, c_ref, r_ref, b_ref, beta_ref, gc_ref):
    g = g_ref[0]
    c = c_ref[...]
    li = g[0:8] + c[0:8]
    xf = g[8:16] + c[8:16]
    lf = jnp.minimum(xf, 0.0) - jnp.log(1.0 + jnp.exp(-jnp.abs(xf)))
    b = _lane_cumsum(lf, lf.shape[1])
    b_ref[0] = b
    r_ref[0] = li - b
    xa = g[16:24] + c[16:24]
    sp = jnp.maximum(xa, 0.0) + jnp.log(1.0 + jnp.exp(-jnp.abs(xa)))
    gdec = -jnp.exp(c[24:32]) * sp
    gc_ref[0] = _lane_cumsum(gdec, CHUNK)
    beta_ref[0] = _sigmoid(g[24:32])


def _gate_prep(gt, consts):
    bsz, _, t = gt.shape
    out = jax.ShapeDtypeStruct((bsz, 8, t), F32)
    spec = pl.BlockSpec((1, 8, t), lambda b: (b, 0, 0))
    return pl.pallas_call(
        _gate_kernel,
        grid=(bsz,),
        in_specs=[pl.BlockSpec((1, GATE_ROWS, t), lambda b: (b, 0, 0)),
                  pl.BlockSpec((GATE_ROWS, 1), lambda b: (0, 0))],
        out_specs=[spec, spec, spec, spec],
        out_shape=[out, out, out, out],
        compiler_params=_params("arbitrary"),
        name="gate_prep",
    )(gt, consts)


def _rotary(t, c, s):
    lane = lax.broadcasted_iota(I32, t.shape, 1)
    swapped = jnp.where((lane % DA_QK_DIM) < DA_QK_DIM // 2,
                        pltpu.roll(t, HEAD - DA_QK_DIM // 2, 1), pltpu.roll(t, DA_QK_DIM // 2, 1))
    return t * c + swapped * s


def _da_kernel(q_ref, k_ref, v_ref, cq_ref, sq_ref, ck_ref, sk_ref, lam_ref, g_ref, o_ref,
               kr_ref, vb_ref, m_ref, l_ref, acc_ref, *, bq, lambda_init):
    qi = pl.program_id(2)

    @pl.when(qi == 0)
    def _():
        kr_ref[...] = _rotary(k_ref[...], ck_ref[...], sk_ref[...]).astype(BF16)
        vb_ref[...] = v_ref[...].astype(BF16)

    qr = _rotary(q_ref[...], cq_ref[...], sq_ref[...]) * (DA_QK_DIM ** -0.5)
    lane = lax.broadcasted_iota(I32, qr.shape, 1)
    qs = jnp.concatenate([jnp.where(lane < DA_QK_DIM, qr, 0.0),
                          jnp.where(lane >= DA_QK_DIM, qr, 0.0)], axis=0).astype(BF16)
    m_ref[...] = jnp.full(m_ref.shape, NEG, F32)
    l_ref[...] = jnp.zeros(l_ref.shape, F32)
    acc_ref[...] = jnp.zeros(acc_ref.shape, F32)

    def block(j, masked):
        off = pl.multiple_of(j * bq, bq)
        s = _dot_nt(qs, kr_ref[pl.ds(off, bq), :])
        if masked:
            r = lax.broadcasted_iota(I32, s.shape, 0)
            c = lax.broadcasted_iota(I32, s.shape, 1)
            s = jnp.where(jnp.where(r >= bq, r - bq, r) >= c, s, NEG)
        m_old = m_ref[...]
        m_new = jnp.maximum(m_old, jnp.max(s, axis=1, keepdims=True))
        p = jnp.exp(s - m_new)
        a = jnp.exp(m_old - m_new)
        l_ref[...] = a * l_ref[...] + jnp.sum(p, axis=1, keepdims=True)
        acc_ref[...] = a * acc_ref[...] + _dot(p.astype(BF16), vb_ref[pl.ds(off, bq), :])
        m_ref[...] = m_new

    def body(j, carry):
        block(j, False)
        return carry

    lax.fori_loop(0, qi, body, 0)
    block(qi, True)

    lam = lam_ref[...]
    s1 = jnp.sum(lam[0:1] * lam[1:2], axis=1, keepdims=True)
    s2 = jnp.sum(lam[2:3] * lam[3:4], axis=1, keepdims=True)
    lam_full = jnp.exp(s1) - jnp.exp(s2) + lambda_init
    acc = acc_ref[...]
    l = l_ref[...]
    o = acc[:bq] / l[:bq] - lam_full * (acc[bq:] / l[bq:])
    o = o * lax.rsqrt(jnp.mean(o * o, axis=1, keepdims=True) + 1e-6) * g_ref[...]
    o_ref[...] = (o * (1.0 - lambda_init)).astype(o_ref.dtype)


def _diff_attention(proj, cos_t, sin_t, lam, norm_g, lambda_init, bsz, t):
    bq = ATT_BLOCK
    nq = t // bq
    kern = functools.partial(_da_kernel, bq=bq, lambda_init=lambda_init)
    tab_q = pl.BlockSpec((bq, HEAD), lambda b, h, i: (i, 0))
    tab_k = pl.BlockSpec((t, HEAD), lambda b, h, i: (0, 0))
    return pl.pallas_call(
        kern,
        grid=(bsz, DA_HEADS, nq),
        in_specs=[pl.BlockSpec((bq, HEAD), lambda b, h, i: (b * nq + i, CB_DA_Q + h)),
                  pl.BlockSpec((t, HEAD), lambda b, h, i: (b, CB_DA_K + h)),
                  pl.BlockSpec((t, HEAD), lambda b, h, i: (b, CB_DA_V + h)),
                  tab_q, tab_q, tab_k, tab_k,
                  pl.BlockSpec((4, DA_QK_DIM), lambda b, h, i: (0, 0)),
                  pl.BlockSpec((1, HEAD), lambda b, h, i: (0, 0))],
        out_specs=pl.BlockSpec((bq, HEAD), lambda b, h, i: (b * nq + i, h)),
        out_shape=jax.ShapeDtypeStruct((bsz * t, DA_HEADS * HEAD), BF16),
        scratch_shapes=[pltpu.VMEM((t, HEAD), BF16), pltpu.VMEM((t, HEAD), BF16),
                        pltpu.VMEM((2 * bq, 1), F32), pltpu.VMEM((2 * bq, 1), F32),
                        pltpu.VMEM((2 * bq, HEAD), F32)],
        compiler_params=_params("arbitrary", "arbitrary", "arbitrary"),
        name="diff_attention",
    )(proj, proj, proj, cos_t, sin_t, cos_t, sin_t, lam, norm_g)


def _ml_kernel(q_ref, k_ref, v_ref, op_ref, b_ref, r_ref, g_ref, o_ref,
               kb_ref, vb_ref, m_ref, den_ref, num_ref, *, bq):
    qi = pl.program_id(2)

    @pl.when(qi == 0)
    def _():
        kb_ref[...] = (k_ref[...] * (HEAD ** -0.5)).astype(BF16)
        vb_ref[...] = v_ref[...].astype(BF16)

    qb = q_ref[...].astype(BF16)
    b_col = _col_from_row(b_ref[0, 0, pl.ds(qi, 1), :], bq)
    m_ref[...] = jnp.full(m_ref.shape, NEG, F32)
    den_ref[...] = jnp.zeros(den_ref.shape, F32)
    num_ref[...] = jnp.zeros(num_ref.shape, F32)

    def block(j, masked):
        off = pl.multiple_of(j * bq, bq)
        s = _dot_nt(qb, kb_ref[pl.ds(off, bq), :])
        logw = b_col + r_ref[0, 0, pl.ds(j, 1), :]
        if masked:
            r = lax.broadcasted_iota(I32, s.shape, 0)
            c = lax.broadcasted_iota(I32, s.shape, 1)
            logw = jnp.where(r >= c, logw, NEG)
        m_old = m_ref[...]
        m_new = jnp.maximum(m_old, jnp.max(logw, axis=1, keepdims=True))
        sw = s * jnp.exp(logw - m_new)
        a = jnp.exp(m_old - m_new)
        den_ref[...] = a * den_ref[...] + jnp.sum(sw, axis=1, keepdims=True)
        num_ref[...] = a * num_ref[...] + _dot(sw.astype(BF16), vb_ref[pl.ds(off, bq), :])
        m_ref[...] = m_new

    def body(j, carry):
        block(j, False)
        return carry

    lax.fori_loop(0, qi, body, 0)
    block(qi, True)

    h = num_ref[...] / jnp.maximum(jnp.abs(den_ref[...]), jnp.exp(-m_ref[...]))
    h = h * lax.rsqrt(jnp.mean(h * h, axis=1, keepdims=True) + 1e-6) * g_ref[...]
    o_ref[...] = (_sigmoid(op_ref[...]) * h).astype(o_ref.dtype)


def _mlstm(proj, b_cum, r_gate, norm_g, bsz, t):
    bq = ATT_BLOCK
    nq = t // bq
    kern = functools.partial(_ml_kernel, bq=bq)
    gate = pl.BlockSpec((1, 1, nq, bq), lambda b, h, i: (b, h, 0, 0))
    return pl.pallas_call(
        kern,
        grid=(bsz, ML_HEADS, nq),
        in_specs=[pl.BlockSpec((bq, HEAD), lambda b, h, i: (b * nq + i, CB_ML_Q + h)),
                  pl.BlockSpec((t, HEAD), lambda b, h, i: (b, CB_ML_K + h)),
                  pl.BlockSpec((t, HEAD), lambda b, h, i: (b, CB_ML_V + h)),
                  pl.BlockSpec((bq, HEAD), lambda b, h, i: (b * nq + i, CB_ML_O + h)),
                  gate, gate,
                  pl.BlockSpec((1, HEAD), lambda b, h, i: (0, 0))],
        out_specs=pl.BlockSpec((bq, HEAD), lambda b, h, i: (b * nq + i, h)),
        out_shape=jax.ShapeDtypeStruct((bsz * t, ML_HEADS * HEAD), BF16),
        scratch_shapes=[pltpu.VMEM((t, HEAD), BF16), pltpu.VMEM((t, HEAD), BF16),
                        pltpu.VMEM((bq, 1), F32), pltpu.VMEM((bq, 1), F32),
                        pltpu.VMEM((bq, HEAD), F32)],
        compiler_params=_params("arbitrary", "arbitrary", "arbitrary"),
        name="mlstm",
    )(proj, proj, proj, proj, b_cum.reshape(bsz, 8, nq, bq), r_gate.reshape(bsz, 8, nq, bq), norm_g)


def _gd_kernel(q_ref, k_ref, v_ref, z_ref, cwq_ref, cwk_ref, cwv_ref, beta_ref, gc_ref, g_ref, o_ref,
               qp_ref, kp_ref, vp_ref, u_ref, w_ref, qg_ref, kd_ref, at_ref, egl_ref, *, t):
    L = CHUNK
    nc = t // L
    pad = 8
    for src, dst in ((q_ref, qp_ref), (k_ref, kp_ref), (v_ref, vp_ref)):
        dst[0:pad, :] = jnp.zeros((pad, HEAD), F32)
        dst[pad:, :] = src[...]

    ri = lax.broadcasted_iota(I32, (L, L), 0)
    ci = lax.broadcasted_iota(I32, (L, L), 1)
    tri = ri >= ci
    strict = ri > ci
    eye = (ri == ci).astype(F32)

    def conv_silu(xp_ref, cw_ref, base):
        blk = xp_ref[pl.ds(base, L + pad), :]
        cw = cw_ref[...]
        y = blk[5:5 + L] * cw[0:1]
        for i in range(1, CONV_W):
            y = y + blk[5 + i:5 + i + L] * cw[i:i + 1]
        return y * _sigmoid(y)

    def mm(a, b):
        return _dot(a.astype(BF16), b.astype(BF16))

    def local(c, carry):
        base = pl.multiple_of(c * L, L)
        q = conv_silu(qp_ref, cwq_ref, base)
        k = conv_silu(kp_ref, cwk_ref, base)
        v = conv_silu(vp_ref, cwv_ref, base)
        q = q * lax.rsqrt(jnp.sum(q * q, axis=1, keepdims=True) + 1e-6) * (HEAD ** -0.5)
        k = k * lax.rsqrt(jnp.sum(k * k, axis=1, keepdims=True) + 1e-6)
        gc_row = gc_ref[0, 0, pl.ds(c, 1), :]
        gc_col = _col_from_row(gc_row, L)
        beta_col = _col_from_row(beta_ref[0, 0, pl.ds(c, 1), :], L)
        decay = jnp.exp(jnp.where(tri, gc_col - gc_row, NEG))
        kb = k * beta_col
        kbf = k.astype(BF16)
        a = jnp.where(strict, _dot_nt(kb.astype(BF16), kbf) * decay, 0.0)
        x = eye - a
        p = mm(a, a)
        x = x + mm(x, p)
        for _ in range(4):
            p = mm(p, p)
            x = x + mm(x, p)
        xb = x.astype(BF16)
        u_ref[pl.ds(base, L), :] = _dot(xb, (v * beta_col).astype(BF16))
        w_ref[pl.ds(base, L), :] = _dot(xb, (kb * jnp.exp(gc_col)).astype(BF16)).astype(BF16)
        attn = jnp.where(tri, _dot_nt(q.astype(BF16), kbf) * decay, 0.0)
        at_ref[pl.ds(base, L), :] = attn.astype(BF16)
        gl = gc_row[:, L - 1:L]
        qg_ref[pl.ds(base, L), :] = (q * jnp.exp(gc_col)).astype(BF16)
        kd_ref[pl.ds(base, L), :] = (k * jnp.exp(gl - gc_col)).astype(BF16)
        egl_ref[pl.ds(c, 1), :] = jnp.broadcast_to(jnp.exp(gl), (1, HEAD))
        return carry

    lax.fori_loop(0, nc, local, 0)

    gn = g_ref[...]

    def step(c, s):
        base = pl.multiple_of(c * L, L)
        sb = s.astype(BF16)
        v_new = u_ref[pl.ds(base, L), :] - _dot(w_ref[pl.ds(base, L), :], sb)
        vnb = v_new.astype(BF16)
        o = _dot(qg_ref[pl.ds(base, L), :], sb) + _dot(at_ref[pl.ds(base, L), :], vnb)
        s = s * egl_ref[pl.ds(c, 1), :] + _dot_tn(kd_ref[pl.ds(base, L), :], vnb)
        o = o * lax.rsqrt(jnp.mean(o * o, axis=1, keepdims=True) + 1e-6) * gn
        z = z_ref[pl.ds(base, L), :]
        o_ref[pl.ds(base, L), :] = (o * (z * _sigmoid(z))).astype(o_ref.dtype)
        return s

    lax.fori_loop(0, nc, step, jnp.zeros((HEAD, HEAD), F32))


def _gated_deltanet(proj, conv_w, beta, gc, norm_g, bsz, t):
    nc = t // CHUNK
    kern = functools.partial(_gd_kernel, t=t)

    def slab(cb):
        return pl.BlockSpec((t, HEAD), lambda b, h: (b, cb + h))

    def cw(off):
        return pl.BlockSpec((CONV_W, HEAD), lambda b, h: (0, off + h))

    gate = pl.BlockSpec((1, 1, nc, CHUNK), lambda b, h: (b, h, 0, 0))
    return pl.pallas_call(
        kern,
        grid=(bsz, GD_HEADS),
        in_specs=[slab(CB_GD_Q), slab(CB_GD_K), slab(CB_GD_V), slab(CB_GD_Z),
                  cw(0), cw(GD_HEADS), cw(2 * GD_HEADS), gate, gate,
                  pl.BlockSpec((1, HEAD), lambda b, h: (0, 0))],
        out_specs=pl.BlockSpec((t, HEAD), lambda b, h: (b, h)),
        out_shape=jax.ShapeDtypeStruct((bsz * t, GD_HEADS * HEAD), BF16),
        scratch_shapes=[pltpu.VMEM((t + 8, HEAD), F32), pltpu.VMEM((t + 8, HEAD), F32),
                        pltpu.VMEM((t + 8, HEAD), F32),
                        pltpu.VMEM((t, HEAD), F32), pltpu.VMEM((t, HEAD), BF16),
                        pltpu.VMEM((t, HEAD), BF16), pltpu.VMEM((t, HEAD), BF16),
                        pltpu.VMEM((t, CHUNK), BF16), pltpu.VMEM((nc, HEAD), F32)],
        compiler_params=_params("arbitrary", "arbitrary"),
        name="gated_deltanet",
    )(proj, proj, proj, proj, conv_w, conv_w, conv_w,
      beta.reshape(bsz, 8, nc, CHUNK), gc.reshape(bsz, 8, nc, CHUNK), norm_g)


def _layer_norm_rows(z, g, b):
    mu = jnp.mean(z, axis=1, keepdims=True)
    zc = z - mu
    var = jnp.mean(zc * zc, axis=1, keepdims=True)
    return zc * lax.rsqrt(var + 1e-5) * g + b


def _proj_ln_kernel(*refs, nx, alpha, tn):
    x_refs = refs[:nx]
    w_refs = refs[nx:2 * nx]
    h_ref, g_ref, b_ref, o_ref, z_ref = refs[2 * nx:]
    for n in range(0, D_MODEL, tn):
        acc = alpha * h_ref[:, n:n + tn]
        for x_ref, w_ref in zip(x_refs, w_refs):
            acc = acc + _dot(x_ref[...], w_ref[:, n:n + tn])
        z_ref[:, n:n + tn] = acc
    o_ref[...] = _layer_norm_rows(z_ref[...], g_ref[...], b_ref[...])


def _proj_ln(xs, ws, h, g, b, alpha, tm=256):
    n_rows = h.shape[0]
    nx = len(xs)
    kern = functools.partial(_proj_ln_kernel, nx=nx, alpha=alpha, tn=512)
    row = lambda i: (i, 0)
    fixed = lambda i: (0, 0)
    in_specs = ([pl.BlockSpec((tm, x.shape[1]), row) for x in xs]
                + [pl.BlockSpec(w.shape, fixed) for w in ws]
                + [pl.BlockSpec((tm, D_MODEL), row),
                   pl.BlockSpec((1, D_MODEL), fixed), pl.BlockSpec((1, D_MODEL), fixed)])
    return pl.pallas_call(
        kern,
        grid=(n_rows // tm,),
        in_specs=in_specs,
        out_specs=pl.BlockSpec((tm, D_MODEL), row),
        out_shape=jax.ShapeDtypeStruct((n_rows, D_MODEL), F32),
        scratch_shapes=[pltpu.VMEM((tm, D_MODEL), F32)],
        compiler_params=_params("arbitrary"),
        name="proj_ln",
    )(*xs, *ws, h, g, b)


def _xa_kernel(q_ref, kv_ref, o_ref):
    for hh in range(XA_HEADS):
        lo = hh * XA_DIM
        s = _dot_nt(q_ref[:, lo:lo + XA_DIM], kv_ref[:, lo:lo + XA_DIM]) * (XA_DIM ** -0.5)
        s = s - jnp.max(s, axis=1, keepdims=True)
        p = jnp.exp(s)
        p = p / jnp.sum(p, axis=1, keepdims=True)
        o = _dot(p.astype(BF16), kv_ref[:, D_MODEL + lo:D_MODEL + lo + XA_DIM])
        o_ref[:, lo:lo + XA_DIM] = o.astype(o_ref.dtype)


def _cross_attention(q, kv, bsz, t, mem_len, tq=512):
    nq = t // tq
    return pl.pallas_call(
        _xa_kernel,
        grid=(bsz, nq),
        in_specs=[pl.BlockSpec((tq, D_MODEL), lambda b, i: (b * nq + i, 0)),
                  pl.BlockSpec((mem_len, 2 * D_MODEL), lambda b, i: (b, 0))],
        out_specs=pl.BlockSpec((tq, D_MODEL), lambda b, i: (b * nq + i, 0)),
        out_shape=jax.ShapeDtypeStruct((bsz * t, D_MODEL), BF16),
        compiler_params=_params("arbitrary", "arbitrary"),
        name="cross_attention",
    )(q, kv)


def _router_kernel(x_ref, rw_ref, bias_ref, ids_ref, wts_ref, cnt_ref, carry_ref, *, tm):
    i = pl.program_id(0)

    @pl.when(i == 0)
    def _():
        carry_ref[...] = jnp.zeros(carry_ref.shape, F32)

    logits = _dot_nt(rw_ref[...], x_ref[...].astype(BF16))
    scores = _sigmoid(logits)
    sel = scores + bias_ref[...]
    srow = [sel[e:e + 1] for e in range(N_EXPERTS)]
    prow = [scores[e:e + 1] for e in range(N_EXPERTS)]

    gscore = []
    for g in range(N_GROUPS):
        s0, s1, s2, s3 = srow[4 * g:4 * g + 4]
        a, b = jnp.maximum(s0, s1), jnp.minimum(s0, s1)
        c, d = jnp.maximum(s2, s3), jnp.minimum(s2, s3)
        gscore.append(jnp.maximum(a, c) + jnp.maximum(jnp.minimum(a, c), jnp.maximum(b, d)))
    gmax = jnp.maximum(jnp.maximum(gscore[0], gscore[1]), jnp.maximum(gscore[2], gscore[3]))
    gi = jnp.where(gscore[0] == gmax, 0, jnp.where(gscore[1] == gmax, 1,
                                                    jnp.where(gscore[2] == gmax, 2, 3)))

    def pick(rows, j):
        return jnp.where(gi == 0, rows[j], jnp.where(gi == 1, rows[4 + j],
                                                      jnp.where(gi == 2, rows[8 + j], rows[12 + j])))

    v = [pick(srow, j) for j in range(GROUP_SIZE)]
    sc = [pick(prow, j) for j in range(GROUP_SIZE)]
    v1 = jnp.maximum(jnp.maximum(v[0], v[1]), jnp.maximum(v[2], v[3]))
    i1 = jnp.where(v[0] == v1, 0, jnp.where(v[1] == v1, 1, jnp.where(v[2] == v1, 2, 3)))
    rest = [jnp.where(i1 == j, -jnp.inf, v[j]) for j in range(GROUP_SIZE)]
    v2 = jnp.maximum(jnp.maximum(rest[0], rest[1]), jnp.maximum(rest[2], rest[3]))
    i2 = jnp.where(rest[0] == v2, 0, jnp.where(rest[1] == v2, 1, jnp.where(rest[2] == v2, 2, 3)))

    def pick_idx(idx):
        return jnp.where(idx == 0, sc[0], jnp.where(idx == 1, sc[1], jnp.where(idx == 2, sc[2], sc[3])))

    p1, p2 = pick_idx(i1), pick_idx(i2)
    tot = p1 + p2
    e1 = 4 * gi + i1
    e2 = 4 * gi + i2

    erow = lax.broadcasted_iota(I32, (N_EXPERTS, tm), 0)
    hit1 = erow == e1
    hit2 = erow == e2
    onehot = jnp.where(hit1 | hit2, 1.0, 0.0)
    incl = _lane_cumsum(onehot, tm)
    excl = incl - onehot + carry_ref[...]
    rank1 = jnp.sum(jnp.where(hit1, excl, 0.0), axis=0, keepdims=True)
    rank2 = jnp.sum(jnp.where(hit2, excl, 0.0), axis=0, keepdims=True)
    carry_ref[...] = carry_ref[...] + incl[:, tm - 1:tm]

    zi = jnp.zeros((4, tm), I32)
    ids_ref[...] = jnp.concatenate([e1, e2, rank1.astype(I32), rank2.astype(I32), zi], axis=0)
    wts_ref[...] = jnp.concatenate([p1 / tot, p2 / tot, jnp.zeros((6, tm), F32)], axis=0)
    cnt_ref[...] = jnp.broadcast_to(carry_ref[...], cnt_ref.shape)


def _router(h, rw_t, bias_col, tm=2048):
    n = h.shape[0]
    tm = min(tm, n)
    kern = functools.partial(_router_kernel, tm=tm)
    return pl.pallas_call(
        kern,
        grid=(n // tm,),
        in_specs=[pl.BlockSpec((tm, D_MODEL), lambda i: (i, 0)),
                  pl.BlockSpec((N_EXPERTS, D_MODEL), lambda i: (0, 0)),
                  pl.BlockSpec((N_EXPERTS, 1), lambda i: (0, 0))],
        out_specs=[pl.BlockSpec((8, tm), lambda i: (0, i)),
                   pl.BlockSpec((8, tm), lambda i: (0, i)),
                   pl.BlockSpec((N_EXPERTS, HEAD), lambda i: (0, 0))],
        out_shape=[jax.ShapeDtypeStruct((8, n), I32), jax.ShapeDtypeStruct((8, n), F32),
                   jax.ShapeDtypeStruct((N_EXPERTS, HEAD), F32)],
        scratch_shapes=[pltpu.VMEM((N_EXPERTS, 1), F32)],
        compiler_params=_params("arbitrary"),
        name="router",
    )(h, rw_t, bias_col)


def _row_copy(src_hbm, row, dst_ref, slot, sem):
    return pltpu.make_async_copy(src_hbm.at[pl.ds(row, 1), :], dst_ref.at[pl.ds(slot, 1), :], sem)


def _expert_kernel(texp_ref, nused_ref, src_ref, h_hbm, w1_ref, w2_ref, o_ref, x_ref, sem, *, tm):
    i = pl.program_id(0)

    @pl.when(i < nused_ref[0])
    def _():
        def start(r, carry):
            _row_copy(h_hbm, src_ref[i * tm + r], x_ref, r, sem.at[r]).start()
            return carry

        lax.fori_loop(0, tm, start, 0)

        def wait(r, carry):
            _row_copy(h_hbm, src_ref[i * tm + r], x_ref, r, sem.at[r]).wait()
            return carry

        lax.fori_loop(0, tm, wait, 0)
        mid = _dot(x_ref[...].astype(BF16), w1_ref[0])
        gate = mid[:, :D_FF]
        act = (gate * _sigmoid(gate)) * mid[:, D_FF:]
        o_ref[...] = _dot(act.astype(BF16), w2_ref[0])

    @pl.when(i >= nused_ref[0])
    def _():
        o_ref[...] = jnp.zeros(o_ref.shape, F32)


def _experts(h, w1, w2, tile_expert, n_used, src_rows, n_tiles):
    tm = MOE_TILE
    kern = functools.partial(_expert_kernel, tm=tm)
    grid_spec = pltpu.PrefetchScalarGridSpec(
        num_scalar_prefetch=3,
        grid=(n_tiles,),
        in_specs=[pl.BlockSpec(memory_space=pl.ANY),
                  pl.BlockSpec((1, D_MODEL, 2 * D_FF), lambda i, te, nu, sr: (te[i], 0, 0)),
                  pl.BlockSpec((1, D_FF, D_MODEL), lambda i, te, nu, sr: (te[i], 0, 0))],
        out_specs=pl.BlockSpec((tm, D_MODEL), lambda i, te, nu, sr: (i, 0)),
        scratch_shapes=[pltpu.VMEM((tm, D_MODEL), F32), pltpu.SemaphoreType.DMA((tm,))],
    )
    return pl.pallas_call(
        kern,
        grid_spec=grid_spec,
        out_shape=jax.ShapeDtypeStruct((n_tiles * tm, D_MODEL), F32),
        compiler_params=_params("arbitrary"),
        name="experts",
    )(tile_expert, n_used, src_rows, h, w1, w2)


def _combine_kernel(p1_ref, p2_ref, y_hbm, h_ref, w1_ref, w2_ref, g_ref, b_ref, o_ref,
                    a_ref, c_ref, sem_a, sem_c, *, tc, alpha):
    i = pl.program_id(0)

    def start(r, carry):
        _row_copy(y_hbm, p1_ref[i * tc + r], a_ref, r, sem_a.at[r]).start()
        _row_copy(y_hbm, p2_ref[i * tc + r], c_ref, r, sem_c.at[r]).start()
        return carry

    lax.fori_loop(0, tc, start, 0)

    def wait(r, carry):
        _row_copy(y_hbm, p1_ref[i * tc + r], a_ref, r, sem_a.at[r]).wait()
        _row_copy(y_hbm, p2_ref[i * tc + r], c_ref, r, sem_c.at[r]).wait()
        return carry

    lax.fori_loop(0, tc, wait, 0)
    z = alpha * h_ref[...] + (w1_ref[...] * a_ref[...] + w2_ref[...] * c_ref[...])
    o_ref[...] = _layer_norm_rows(z, g_ref[...], b_ref[...])


def _combine(y, h, pos1, pos2, w1c, w2c, g, b, alpha, tc=256):
    n = h.shape[0]
    kern = functools.partial(_combine_kernel, tc=tc, alpha=alpha)
    row = lambda i, p1, p2: (i, 0)
    fixed = lambda i, p1, p2: (0, 0)
    grid_spec = pltpu.PrefetchScalarGridSpec(
        num_scalar_prefetch=2,
        grid=(n // tc,),
        in_specs=[pl.BlockSpec(memory_space=pl.ANY),
                  pl.BlockSpec((tc, D_MODEL), row),
                  pl.BlockSpec((tc, 1), row), pl.BlockSpec((tc, 1), row),
                  pl.BlockSpec((1, D_MODEL), fixed), pl.BlockSpec((1, D_MODEL), fixed)],
        out_specs=pl.BlockSpec((tc, D_MODEL), row),
        scratch_shapes=[pltpu.VMEM((tc, D_MODEL), F32), pltpu.VMEM((tc, D_MODEL), F32),
                        pltpu.SemaphoreType.DMA((tc,)), pltpu.SemaphoreType.DMA((tc,))],
    )
    return pl.pallas_call(
        kern,
        grid_spec=grid_spec,
        out_shape=jax.ShapeDtypeStruct((n, D_MODEL), F32),
        compiler_params=_params("arbitrary"),
        name="moe_combine",
    )(pos1, pos2, y, h, w1c, w2c, g, b)


def _moe(h, rw_t, bias_col, w1, w2, g, b, alpha):
    n = h.shape[0]
    tm = MOE_TILE
    n_tiles = (2 * n) // tm + N_EXPERTS
    ids, wts, cnt = _router(h, rw_t, bias_col)
    counts = cnt[:, 0].astype(I32)
    padded = ((counts + tm - 1) // tm) * tm
    ends = jnp.cumsum(padded)
    offs = ends - padded
    pos1 = offs[ids[0]] + ids[2]
    pos2 = offs[ids[1]] + ids[3]
    tok = jnp.arange(n, dtype=I32)
    src_rows = jnp.zeros((n_tiles * tm,), I32).at[pos1].set(tok).at[pos2].set(tok)
    tile_start = jnp.arange(n_tiles, dtype=I32) * tm
    tile_expert = jnp.minimum(jnp.sum(tile_start[:, None] >= ends[None, :], axis=1),
                              N_EXPERTS - 1).astype(I32)
    n_used = (ends[-1:] // tm).astype(I32)
    y = _experts(h, w1, w2, tile_expert, n_used, src_rows, n_tiles)
    return _combine(y, h, pos1, pos2, wts[0].reshape(n, 1), wts[1].reshape(n, 1), g, b, alpha)


def _rotary_tables(t):
    inv = 1.0 / (ROPE_THETA ** (jnp.arange(0, DA_QK_DIM, 2, dtype=F32) / DA_QK_DIM))
    ang = jnp.arange(t, dtype=F32)[:, None] * inv[None, :]
    cos, sin = jnp.cos(ang), jnp.sin(ang)
    return jnp.tile(cos, (1, 4)), jnp.concatenate([-sin, sin, -sin, sin], axis=1)


def _split_w_in(w):
    main = jnp.concatenate([w[:, :3584], w[:, 3592:7688]], axis=1).astype(BF16)
    zero4 = jnp.zeros((w.shape[0], 4), w.dtype)
    gate = jnp.concatenate([w[:, 3584:3588], zero4, w[:, 3588:3592], zero4, w[:, 7688:7704],
                            jnp.zeros((w.shape[0], HEAD - GATE_ROWS), w.dtype)], axis=1).astype(BF16)
    return main, gate


def kernel(x, mem, w_in, da_lambda, da_norm_g, ml_gate_b, ml_norm_g, gd_conv_w, gd_a_log, gd_dt_bias,
           gd_norm_g, w_out, xa_wq, xa_wkv, xa_wo, router_w, router_bias, moe_w_in, moe_w_out,
           ln_g, ln_b):
    bsz, t, d = x.shape
    mem_len = mem.shape[1]
    depth = w_in.shape[0]
    n = bsz * t
    alpha = (2 * depth) ** 0.25
    cos_t, sin_t = _rotary_tables(t)
    rw_t = router_w.T.astype(BF16)
    bias_col = router_bias.reshape(N_EXPERTS, 1).astype(F32)
    mem2 = mem.reshape(bsz * mem_len, d)
    zero4 = jnp.zeros((4,), F32)
    h = x.reshape(n, d)
    for l in range(depth):
        lambda_init = 0.8 - 0.6 * math.exp(-0.3 * l)
        w_main, w_gate = _split_w_in(w_in[l])
        proj = _matmul(h, w_main, F32, 1024, 512)
        gates = _matmul(h, w_gate, F32, 1024, HEAD)
        gt = gates[:, :GATE_ROWS].reshape(bsz, t, GATE_ROWS).transpose(0, 2, 1)
        consts = jnp.concatenate([ml_gate_b[l, 0], zero4, ml_gate_b[l, 1], zero4,
                                  gd_dt_bias[l], gd_a_log[l]]).reshape(GATE_ROWS, 1)
        r_gate, b_cum, beta, gc = _gate_prep(gt, consts)
        y_da = _diff_attention(proj, cos_t, sin_t, da_lambda[l], da_norm_g[l].reshape(1, HEAD),
                               lambda_init, bsz, t)
        y_ml = _mlstm(proj, b_cum, r_gate, ml_norm_g[l].reshape(1, HEAD), bsz, t)
        y_gd = _gated_deltanet(proj, gd_conv_w[l], beta, gc, gd_norm_g[l].reshape(1, HEAD), bsz, t)
        wo = w_out[l].astype(BF16)
        h = _proj_ln([y_da, y_ml, y_gd], [wo[:512], wo[512:1024], wo[1024:]], h,
                     ln_g[l, 0].reshape(1, d), ln_b[l, 0].reshape(1, d), alpha)
        q = _matmul(h, xa_wq[l].astype(BF16), BF16, 1024, 512)
        kv = _matmul(mem2, xa_wkv[l].astype(BF16), BF16, 1024, 512)
        xo = _cross_attention(q, kv, bsz, t, mem_len)
        h = _proj_ln([xo], [xa_wo[l].astype(BF16)], h,
                     ln_g[l, 1].reshape(1, d), ln_b[l, 1].reshape(1, d), alpha)
        h = _moe(h, rw_t, bias_col, moe_w_in[l].astype(BF16), moe_w_out[l].astype(BF16),
                 ln_g[l, 2].reshape(1, d), ln_b[l, 2].reshape(1, d), alpha)
    return h.reshape(bsz, t, d)
```

```python
import functools
import math

import jax
import jax.numpy as jnp
from jax import lax
from jax.experimental import pallas as pl
from jax.experimental.pallas import tpu as pltpu

F32 = jnp.float32
BF16 = jnp.bfloat16
I32 = jnp.int32

D_MODEL = 2048
HEAD = 128
DA_HEADS = 4
DA_QK_DIM = 64
ML_HEADS = 4
GD_HEADS = 8
CHUNK = 64
CONV_W = 4
ROPE_THETA = 10000.0
XA_HEADS = 4
XA_DIM = D_MODEL // XA_HEADS
N_EXPERTS = 16
N_GROUPS = 4
GROUP_SIZE = 4
D_FF = D_MODEL // 2
MAIN_WIDTH = 7680
GATE_ROWS = 32
NEG = -1e30
VMEM_LIMIT = 56 * 1024 * 1024
MOE_TILE = 256
ATT_BLOCK = 256

CB_DA_Q, CB_DA_K, CB_DA_V = 0, 4, 8
CB_ML_Q, CB_ML_K, CB_ML_V, CB_ML_O = 12, 16, 20, 24
CB_GD_Q, CB_GD_K, CB_GD_V, CB_GD_Z = 28, 36, 44, 52


def _params(*sem):
    return pltpu.CompilerParams(dimension_semantics=sem, vmem_limit_bytes=VMEM_LIMIT)


def _dot(a, b):
    return jnp.dot(a, b, preferred_element_type=F32)


def _dot_nt(a, b):
    return lax.dot_general(a, b, (((1,), (1,)), ((), ())), preferred_element_type=F32)


def _dot_tn(a, b):
    return lax.dot_general(a, b, (((0,), (0,)), ((), ())), preferred_element_type=F32)


def _sigmoid(x):
    return 1.0 / (1.0 + jnp.exp(-x))


def _col_from_row(row, n):
    ri = lax.broadcasted_iota(I32, (n, n), 0)
    ci = lax.broadcasted_iota(I32, (n, n), 1)
    return jnp.sum(jnp.where(ri == ci, jnp.broadcast_to(row, (n, n)), 0.0), axis=1, keepdims=True)


def _lane_cumsum(x, seg):
    lane = lax.broadcasted_iota(I32, x.shape, 1) % seg
    s = 1
    while s < seg:
        x = x + jnp.where(lane >= s, pltpu.roll(x, s, 1), 0.0)
        s *= 2
    return x


def _mm_kernel(x_ref, w_ref, o_ref, xb_ref):
    @pl.when(pl.program_id(1) == 0)
    def _():
        xb_ref[...] = x_ref[...].astype(BF16)

    o_ref[...] = _dot(xb_ref[...], w_ref[...]).astype(o_ref.dtype)


def _matmul(x, w, out_dtype, tm, tn):
    m, k = x.shape
    n = w.shape[1]
    tm = min(tm, m)
    return pl.pallas_call(
        _mm_kernel,
        grid=(m // tm, n // tn),
        in_specs=[pl.BlockSpec((tm, k), lambda i, j: (i, 0)),
                  pl.BlockSpec((k, tn), lambda i, j: (0, j))],
        out_specs=pl.BlockSpec((tm, tn), lambda i, j: (i, j)),
        out_shape=jax.ShapeDtypeStruct((m, n), out_dtype),
        scratch_shapes=[pltpu.VMEM((tm, k), BF16)],
        compiler_params=_params("arbitrary", "arbitrary"),
        name="matmul",
    )(x, w)


def _lane_cummax(x):
    lane = lax.broadcasted_iota(I32, x.shape, 1)
    s = 1
    while s < x.shape[1]:
        x = jnp.maximum(x, jnp.where(lane >= s, pltpu.roll(x, s, 1), NEG))
        s *= 2
    return x


def _gate_kernel(g_ref, c_ref, r_ref, b_ref, rmax_ref, beta_ref, gc_ref):
    g = g_ref[0]
    c = c_ref[...]
    li = g[0:8] + c[0:8]
    xf = g[8:16] + c[8:16]
    lf = jnp.minimum(xf, 0.0) - jnp.log(1.0 + jnp.exp(-jnp.abs(xf)))
    b = _lane_cumsum(lf, lf.shape[1])
    b_ref[0] = b
    r = li - b
    r_ref[0] = r
    rmax_ref[0] = _lane_cummax(r)
    xa = g[16:24] + c[16:24]
    sp = jnp.maximum(xa, 0.0) + jnp.log(1.0 + jnp.exp(-jnp.abs(xa)))
    gdec = -jnp.exp(c[24:32]) * sp
    gc_ref[0] = _lane_cumsum(gdec, CHUNK)
    beta_ref[0] = _sigmoid(g[24:32])


def _gate_prep(gt, consts):
    bsz, _, t = gt.shape
    out = jax.ShapeDtypeStruct((bsz, 8, t), F32)
    spec = pl.BlockSpec((1, 8, t), lambda b: (b, 0, 0))
    return pl.pallas_call(
        _gate_kernel,
        grid=(bsz,),
        in_specs=[pl.BlockSpec((1, GATE_ROWS, t), lambda b: (b, 0, 0)),
                  pl.BlockSpec((GATE_ROWS, 1), lambda b: (0, 0))],
        out_specs=[spec] * 5,
        out_shape=[out] * 5,
        compiler_params=_params("arbitrary"),
        name="gate_prep",
    )(gt, consts)


def _rotary(t, c, s):
    lane = lax.broadcasted_iota(I32, t.shape, 1)
    swapped = jnp.where((lane % DA_QK_DIM) < DA_QK_DIM // 2,
                        pltpu.roll(t, HEAD - DA_QK_DIM // 2, 1), pltpu.roll(t, DA_QK_DIM // 2, 1))
    return t * c + swapped * s


def _da_kernel(q_ref, k_ref, v_ref, cq_ref, sq_ref, ck_ref, sk_ref, lam_ref, g_ref, o_ref,
               kr_ref, vb_ref, s_ref, mx_ref, ls_ref, acc_ref, *, bq, lambda_init):
    qi = pl.program_id(2)
    nh = bq // HEAD

    @pl.when(qi == 0)
    def _():
        kr_ref[...] = _rotary(k_ref[...], ck_ref[...], sk_ref[...]).astype(BF16)
        vb_ref[...] = v_ref[...].astype(BF16)

    qr = _rotary(q_ref[...], cq_ref[...], sq_ref[...]) * (DA_QK_DIM ** -0.5)
    lane = lax.broadcasted_iota(I32, qr.shape, 1)
    qs = jnp.concatenate([jnp.where(lane < DA_QK_DIM, qr, 0.0),
                          jnp.where(lane >= DA_QK_DIM, qr, 0.0)], axis=0).astype(BF16)

    def fold(x, op):
        out = x[:, :HEAD]
        for i in range(1, nh):
            out = op(out, x[:, i * HEAD:(i + 1) * HEAD])
        return out

    mx_ref[...] = jnp.full(mx_ref.shape, NEG, F32)

    def scores(j, carry):
        off = pl.multiple_of(j * bq, bq)
        s = _dot_nt(qs, kr_ref[pl.ds(off, bq), :])
        s_ref[j] = s
        mx_ref[...] = jnp.maximum(mx_ref[...], fold(s, jnp.maximum))
        return carry

    lax.fori_loop(0, qi, scores, 0)
    offd = pl.multiple_of(qi * bq, bq)
    sd = _dot_nt(qs, kr_ref[pl.ds(offd, bq), :])
    r = lax.broadcasted_iota(I32, sd.shape, 0)
    c = lax.broadcasted_iota(I32, sd.shape, 1)
    sd = jnp.where(jnp.where(r >= bq, r - bq, r) >= c, sd, NEG)
    m = jnp.max(jnp.maximum(mx_ref[...], fold(sd, jnp.maximum)), axis=1, keepdims=True)
    mb = jnp.broadcast_to(m, (2 * bq, HEAD))

    ls_ref[...] = jnp.zeros(ls_ref.shape, F32)
    acc_ref[...] = jnp.zeros(acc_ref.shape, F32)

    def accumulate(s, off):
        ps = [jnp.exp(s[:, i * HEAD:(i + 1) * HEAD] - mb) for i in range(nh)]
        tot = ps[0]
        for p in ps[1:]:
            tot = tot + p
        ls_ref[...] += tot
        pb = jnp.concatenate([p.astype(BF16) for p in ps], axis=1)
        acc_ref[...] += _dot(pb, vb_ref[pl.ds(off, bq), :])

    def weights(j, carry):
        accumulate(s_ref[j], pl.multiple_of(j * bq, bq))
        return carry

    lax.fori_loop(0, qi, weights, 0)
    accumulate(sd, offd)

    lam = lam_ref[...]
    s1 = jnp.sum(lam[0:1] * lam[1:2], axis=1, keepdims=True)
    s2 = jnp.sum(lam[2:3] * lam[3:4], axis=1, keepdims=True)
    lam_full = jnp.exp(s1) - jnp.exp(s2) + lambda_init
    acc = acc_ref[...]
    l = jnp.sum(ls_ref[...], axis=1, keepdims=True)
    o = acc[:bq] / l[:bq] - lam_full * (acc[bq:] / l[bq:])
    o = o * lax.rsqrt(jnp.mean(o * o, axis=1, keepdims=True) + 1e-6) * g_ref[...]
    o_ref[...] = (o * (1.0 - lambda_init)).astype(o_ref.dtype)


def _diff_attention(proj, cos_t, sin_t, lam, norm_g, lambda_init, bsz, t):
    bq = ATT_BLOCK
    nq = t // bq
    kern = functools.partial(_da_kernel, bq=bq, lambda_init=lambda_init)
    tab_q = pl.BlockSpec((bq, HEAD), lambda b, h, i: (i, 0))
    tab_k = pl.BlockSpec((t, HEAD), lambda b, h, i: (0, 0))
    return pl.pallas_call(
        kern,
        grid=(bsz, DA_HEADS, nq),
        in_specs=[pl.BlockSpec((bq, HEAD), lambda b, h, i: (b * nq + i, CB_DA_Q + h)),
                  pl.BlockSpec((t, HEAD), lambda b, h, i: (b, CB_DA_K + h)),
                  pl.BlockSpec((t, HEAD), lambda b, h, i: (b, CB_DA_V + h)),
                  tab_q, tab_q, tab_k, tab_k,
                  pl.BlockSpec((4, DA_QK_DIM), lambda b, h, i: (0, 0)),
                  pl.BlockSpec((1, HEAD), lambda b, h, i: (0, 0))],
        out_specs=pl.BlockSpec((bq, HEAD), lambda b, h, i: (b * nq + i, h)),
        out_shape=jax.ShapeDtypeStruct((bsz * t, DA_HEADS * HEAD), BF16),
        scratch_shapes=[pltpu.VMEM((t, HEAD), BF16), pltpu.VMEM((t, HEAD), BF16),
                        pltpu.VMEM((nq, 2 * bq, bq), F32),
                        pltpu.VMEM((2 * bq, HEAD), F32), pltpu.VMEM((2 * bq, HEAD), F32),
                        pltpu.VMEM((2 * bq, HEAD), F32)],
        compiler_params=_params("arbitrary", "arbitrary", "arbitrary"),
        name="diff_attention",
    )(proj, proj, proj, cos_t, sin_t, cos_t, sin_t, lam, norm_g)


def _ml_kernel(q_ref, k_ref, v_ref, op_ref, b_ref, r_ref, rmax_ref, g_ref, o_ref,
               kb_ref, vb_ref, den_ref, num_ref, *, bq):
    qi = pl.program_id(2)
    nh = bq // HEAD

    @pl.when(qi == 0)
    def _():
        kb_ref[...] = (k_ref[...] * (HEAD ** -0.5)).astype(BF16)
        vb_ref[...] = v_ref[...].astype(BF16)

    qb = q_ref[...].astype(BF16)
    b_col = _col_from_row(b_ref[0, 0, pl.ds(qi, 1), :], bq)
    rmax_col = _col_from_row(rmax_ref[0, 0, pl.ds(qi, 1), :], bq)
    den_ref[...] = jnp.zeros(den_ref.shape, F32)
    num_ref[...] = jnp.zeros(num_ref.shape, F32)

    def block(j, masked):
        off = pl.multiple_of(j * bq, bq)
        s = _dot_nt(qb, kb_ref[pl.ds(off, bq), :])
        logw = r_ref[0, 0, pl.ds(j, 1), :] - rmax_col
        if masked:
            r = lax.broadcasted_iota(I32, s.shape, 0)
            c = lax.broadcasted_iota(I32, s.shape, 1)
            logw = jnp.where(r >= c, logw, NEG)
        sw = s * jnp.exp(logw)
        tot = sw[:, :HEAD]
        for i in range(1, nh):
            tot = tot + sw[:, i * HEAD:(i + 1) * HEAD]
        den_ref[...] += tot
        num_ref[...] += _dot(sw.astype(BF16), vb_ref[pl.ds(off, bq), :])

    def body(j, carry):
        block(j, False)
        return carry

    lax.fori_loop(0, qi, body, 0)
    block(qi, True)

    den = jnp.sum(den_ref[...], axis=1, keepdims=True)
    h = num_ref[...] / jnp.maximum(jnp.abs(den), jnp.exp(-(b_col + rmax_col)))
    h = h * lax.rsqrt(jnp.mean(h * h, axis=1, keepdims=True) + 1e-6) * g_ref[...]
    o_ref[...] = (_sigmoid(op_ref[...]) * h).astype(o_ref.dtype)


def _mlstm(proj, b_cum, r_gate, r_max, norm_g, bsz, t):
    bq = ATT_BLOCK
    nq = t // bq
    kern = functools.partial(_ml_kernel, bq=bq)
    gate = pl.BlockSpec((1, 1, nq, bq), lambda b, h, i: (b, h, 0, 0))
    return pl.pallas_call(
        kern,
        grid=(bsz, ML_HEADS, nq),
        in_specs=[pl.BlockSpec((bq, HEAD), lambda b, h, i: (b * nq + i, CB_ML_Q + h)),
                  pl.BlockSpec((t, HEAD), lambda b, h, i: (b, CB_ML_K + h)),
                  pl.BlockSpec((t, HEAD), lambda b, h, i: (b, CB_ML_V + h)),
                  pl.BlockSpec((bq, HEAD), lambda b, h, i: (b * nq + i, CB_ML_O + h)),
                  gate, gate, gate,
                  pl.BlockSpec((1, HEAD), lambda b, h, i: (0, 0))],
        out_specs=pl.BlockSpec((bq, HEAD), lambda b, h, i: (b * nq + i, h)),
        out_shape=jax.ShapeDtypeStruct((bsz * t, ML_HEADS * HEAD), BF16),
        scratch_shapes=[pltpu.VMEM((t, HEAD), BF16), pltpu.VMEM((t, HEAD), BF16),
                        pltpu.VMEM((bq, HEAD), F32), pltpu.VMEM((bq, HEAD), F32)],
        compiler_params=_params("arbitrary", "arbitrary", "arbitrary"),
        name="mlstm",
    )(proj, proj, proj, proj, b_cum.reshape(bsz, 8, nq, bq), r_gate.reshape(bsz, 8, nq, bq),
      r_max.reshape(bsz, 8, nq, bq), norm_g)


def _gd_kernel(q_ref, k_ref, v_ref, z_ref, cwq_ref, cwk_ref, cwv_ref, beta_ref, gc_ref, g_ref, o_ref,
               mq_ref, no_ref, egl_ref, *, t, heads):
    L = CHUNK
    S2 = 2 * L
    nc = t // L
    ri = lax.broadcasted_iota(I32, (S2, S2), 0)
    ci = lax.broadcasted_iota(I32, (S2, S2), 1)
    same = (ri // L) == (ci // L)
    tri = same & (ri >= ci)
    strict = same & (ri > ci)
    eye = (ri == ci).astype(F32)
    first_rows = lax.broadcasted_iota(I32, (S2, 1), 0) < L

    def conv_silu(x_ref, cw_ref, lo, sc, base):
        cur = x_ref[pl.ds(base, S2), lo:lo + HEAD]
        prev = x_ref[pl.ds(pl.multiple_of(jnp.maximum(base - 8, 0), 8), 8), lo:lo + HEAD]
        blk = jnp.concatenate([jnp.where(sc > 0, prev, 0.0), cur], axis=0)
        cw = cw_ref[:, lo:lo + HEAD]
        y = blk[5:5 + S2] * cw[0:1]
        for i in range(1, CONV_W):
            y = y + blk[5 + i:5 + i + S2] * cw[i:i + 1]
        return y * _sigmoid(y)

    def mm(a, b):
        return _dot(a.astype(BF16), b.astype(BF16))

    def local_head(hh, sc):
        lo = hh * HEAD
        base = pl.multiple_of(sc * S2, S2)
        q = conv_silu(q_ref, cwq_ref, lo, sc, base)
        k = conv_silu(k_ref, cwk_ref, lo, sc, base)
        v = conv_silu(v_ref, cwv_ref, lo, sc, base)
        q = q * lax.rsqrt(jnp.sum(q * q, axis=1, keepdims=True) + 1e-6) * (HEAD ** -0.5)
        k = k * lax.rsqrt(jnp.sum(k * k, axis=1, keepdims=True) + 1e-6)
        gc_row = gc_ref[0, hh, pl.ds(sc, 1), :]
        gc_col = _col_from_row(gc_row, S2)
        beta_col = _col_from_row(beta_ref[0, hh, pl.ds(sc, 1), :], S2)
        decay = jnp.exp(jnp.where(tri, gc_col - gc_row, NEG))
        kb = k * beta_col
        kq = _dot_nt(jnp.concatenate([kb, q], axis=0).astype(BF16), k.astype(BF16))
        a = jnp.where(strict, kq[:S2] * decay, 0.0)
        attn = jnp.where(tri, kq[S2:] * decay, 0.0).astype(BF16)
        x = eye - a
        p = mm(a, a)
        for _ in range(4):
            px = mm(jnp.concatenate([p, x], axis=0), p)
            p = px[:S2]
            x = x + px[S2:]
        x = x + mm(x, p)
        rhs = jnp.concatenate([kb * jnp.exp(gc_col), v * beta_col], axis=1)
        wu = mm(x, rhs).astype(BF16)
        gl0 = gc_row[:, L - 1:L]
        gl1 = gc_row[:, S2 - 1:S2]
        gl_col = jnp.where(first_rows, gl0, gl1)
        kd = (k * jnp.exp(gl_col - gc_col)).astype(BF16)
        aw = _dot(attn, wu)
        qprime = (q * jnp.exp(gc_col) - aw[:, :HEAD]).astype(BF16)
        for c, gl in ((0, gl0), (1, gl1)):
            rows = slice(c * L, (c + 1) * L)
            mn = _dot_tn(kd[rows], wu[rows])
            blk = pl.multiple_of((2 * sc + c) * (HEAD + L), L)
            mq_ref[hh, pl.ds(blk, HEAD), :] = mn[:, :HEAD].astype(BF16)
            mq_ref[hh, pl.ds(blk + HEAD, L), :] = qprime[rows]
            no_ref[hh, pl.ds(blk, HEAD), :] = mn[:, HEAD:]
            no_ref[hh, pl.ds(blk + HEAD, L), :] = aw[rows, HEAD:]
            egl_ref[hh, pl.ds(2 * sc + c, 1), :] = jnp.broadcast_to(jnp.exp(gl), (1, HEAD))

    def local(sc, carry):
        for hh in range(heads):
            local_head(hh, sc)
        return carry

    lax.fori_loop(0, nc // 2, local, 0, unroll=2)

    gn = g_ref[...]

    def step(c, states):
        base = pl.multiple_of(c * L, L)
        blk = pl.multiple_of(c * (HEAD + L), L)
        new = []
        for hh in range(heads):
            lo = hh * HEAD
            s = states[hh]
            ms = _dot(mq_ref[hh, pl.ds(blk, HEAD + L), :], s.astype(BF16))
            new.append(s * egl_ref[hh, pl.ds(c, 1), :] - ms[:HEAD] + no_ref[hh, pl.ds(blk, HEAD), :])
            o = ms[HEAD:] + no_ref[hh, pl.ds(blk + HEAD, L), :]
            o = o * lax.rsqrt(jnp.mean(o * o, axis=1, keepdims=True) + 1e-6) * gn
            z = z_ref[pl.ds(base, L), lo:lo + HEAD]
            o_ref[pl.ds(base, L), lo:lo + HEAD] = (o * (z * _sigmoid(z))).astype(o_ref.dtype)
        return tuple(new)

    lax.fori_loop(0, nc, step, tuple(jnp.zeros((HEAD, HEAD), F32) for _ in range(heads)), unroll=2)


def _gated_deltanet(proj, conv_w, beta, gc, norm_g, bsz, t, heads=2):
    nc = t // CHUNK
    width = heads * HEAD
    kern = functools.partial(_gd_kernel, t=t, heads=heads)

    def slab(cb):
        return pl.BlockSpec((t, width), lambda b, h: (b, cb // heads + h))

    def cw(off):
        return pl.BlockSpec((CONV_W, width), lambda b, h: (0, off // heads + h))

    gate = pl.BlockSpec((1, heads, nc // 2, 2 * CHUNK), lambda b, h: (b, h, 0, 0))
    return pl.pallas_call(
        kern,
        grid=(bsz, GD_HEADS // heads),
        in_specs=[slab(CB_GD_Q), slab(CB_GD_K), slab(CB_GD_V), slab(CB_GD_Z),
                  cw(0), cw(GD_HEADS), cw(2 * GD_HEADS), gate, gate,
                  pl.BlockSpec((1, HEAD), lambda b, h: (0, 0))],
        out_specs=pl.BlockSpec((t, width), lambda b, h: (b, h)),
        out_shape=jax.ShapeDtypeStruct((bsz * t, GD_HEADS * HEAD), BF16),
        scratch_shapes=[pltpu.VMEM((heads, nc * (HEAD + CHUNK), HEAD), BF16),
                        pltpu.VMEM((heads, nc * (HEAD + CHUNK), HEAD), F32),
                        pltpu.VMEM((heads, nc, HEAD), F32)],
        compiler_params=_params("arbitrary", "arbitrary"),
        name="gated_deltanet",
    )(proj, proj, proj, proj, conv_w, conv_w, conv_w,
      beta.reshape(bsz, 8, nc // 2, 2 * CHUNK), gc.reshape(bsz, 8, nc // 2, 2 * CHUNK), norm_g)


def _layer_norm_rows(z, g, b):
    mu = jnp.mean(z, axis=1, keepdims=True)
    zc = z - mu
    var = jnp.mean(zc * zc, axis=1, keepdims=True)
    return zc * lax.rsqrt(var + 1e-5) * g + b


def _proj_ln_kernel(*refs, nx, alpha, tn):
    x_refs = refs[:nx]
    w_refs = refs[nx:2 * nx]
    h_ref, g_ref, b_ref, o_ref, z_ref = refs[2 * nx:]
    for n in range(0, D_MODEL, tn):
        acc = alpha * h_ref[:, n:n + tn]
        for x_ref, w_ref in zip(x_refs, w_refs):
            acc = acc + _dot(x_ref[...], w_ref[:, n:n + tn])
        z_ref[:, n:n + tn] = acc
    o_ref[...] = _layer_norm_rows(z_ref[...], g_ref[...], b_ref[...])


def _proj_ln(xs, ws, h, g, b, alpha, tm=256):
    n_rows = h.shape[0]
    nx = len(xs)
    kern = functools.partial(_proj_ln_kernel, nx=nx, alpha=alpha, tn=512)
    row = lambda i: (i, 0)
    fixed = lambda i: (0, 0)
    in_specs = ([pl.BlockSpec((tm, x.shape[1]), row) for x in xs]
                + [pl.BlockSpec(w.shape, fixed) for w in ws]
                + [pl.BlockSpec((tm, D_MODEL), row),
                   pl.BlockSpec((1, D_MODEL), fixed), pl.BlockSpec((1, D_MODEL), fixed)])
    return pl.pallas_call(
        kern,
        grid=(n_rows // tm,),
        in_specs=in_specs,
        out_specs=pl.BlockSpec((tm, D_MODEL), row),
        out_shape=jax.ShapeDtypeStruct((n_rows, D_MODEL), F32),
        scratch_shapes=[pltpu.VMEM((tm, D_MODEL), F32)],
        compiler_params=_params("arbitrary"),
        name="proj_ln",
    )(*xs, *ws, h, g, b)


def _xa_kernel(q_ref, kv_ref, o_ref):
    for hh in range(XA_HEADS):
        lo = hh * XA_DIM
        s = _dot_nt(q_ref[:, lo:lo + XA_DIM], kv_ref[:, lo:lo + XA_DIM]) * (XA_DIM ** -0.5)
        s = s - jnp.max(s, axis=1, keepdims=True)
        p = jnp.exp(s)
        p = p / jnp.sum(p, axis=1, keepdims=True)
        o = _dot(p.astype(BF16), kv_ref[:, D_MODEL + lo:D_MODEL + lo + XA_DIM])
        o_ref[:, lo:lo + XA_DIM] = o.astype(o_ref.dtype)


def _cross_attention(q, kv, bsz, t, mem_len, tq=512):
    nq = t // tq
    return pl.pallas_call(
        _xa_kernel,
        grid=(bsz, nq),
        in_specs=[pl.BlockSpec((tq, D_MODEL), lambda b, i: (b * nq + i, 0)),
                  pl.BlockSpec((mem_len, 2 * D_MODEL), lambda b, i: (b, 0))],
        out_specs=pl.BlockSpec((tq, D_MODEL), lambda b, i: (b * nq + i, 0)),
        out_shape=jax.ShapeDtypeStruct((bsz * t, D_MODEL), BF16),
        compiler_params=_params("arbitrary", "arbitrary"),
        name="cross_attention",
    )(q, kv)


def _router_kernel(x_ref, rw_ref, bias_ref, ids_ref, wts_ref, cnt_ref, carry_ref, *, tm):
    i = pl.program_id(0)

    @pl.when(i == 0)
    def _():
        carry_ref[...] = jnp.zeros(carry_ref.shape, F32)

    logits = _dot_nt(rw_ref[...], x_ref[...].astype(BF16))
    scores = _sigmoid(logits)
    sel = scores + bias_ref[...]
    srow = [sel[e:e + 1] for e in range(N_EXPERTS)]
    prow = [scores[e:e + 1] for e in range(N_EXPERTS)]

    gscore = []
    for g in range(N_GROUPS):
        s0, s1, s2, s3 = srow[4 * g:4 * g + 4]
        a, b = jnp.maximum(s0, s1), jnp.minimum(s0, s1)
        c, d = jnp.maximum(s2, s3), jnp.minimum(s2, s3)
        gscore.append(jnp.maximum(a, c) + jnp.maximum(jnp.minimum(a, c), jnp.maximum(b, d)))
    gmax = jnp.maximum(jnp.maximum(gscore[0], gscore[1]), jnp.maximum(gscore[2], gscore[3]))
    gi = jnp.where(gscore[0] == gmax, 0, jnp.where(gscore[1] == gmax, 1,
                                                    jnp.where(gscore[2] == gmax, 2, 3)))

    def pick(rows, j):
        return jnp.where(gi == 0, rows[j], jnp.where(gi == 1, rows[4 + j],
                                                      jnp.where(gi == 2, rows[8 + j], rows[12 + j])))

    v = [pick(srow, j) for j in range(GROUP_SIZE)]
    sc = [pick(prow, j) for j in range(GROUP_SIZE)]
    v1 = jnp.maximum(jnp.maximum(v[0], v[1]), jnp.maximum(v[2], v[3]))
    i1 = jnp.where(v[0] == v1, 0, jnp.where(v[1] == v1, 1, jnp.where(v[2] == v1, 2, 3)))
    rest = [jnp.where(i1 == j, -jnp.inf, v[j]) for j in range(GROUP_SIZE)]
    v2 = jnp.maximum(jnp.maximum(rest[0], rest[1]), jnp.maximum(rest[2], rest[3]))
    i2 = jnp.where(rest[0] == v2, 0, jnp.where(rest[1] == v2, 1, jnp.where(rest[2] == v2, 2, 3)))

    def pick_idx(idx):
        return jnp.where(idx == 0, sc[0], jnp.where(idx == 1, sc[1], jnp.where(idx == 2, sc[2], sc[3])))

    p1, p2 = pick_idx(i1), pick_idx(i2)
    tot = p1 + p2
    e1 = 4 * gi + i1
    e2 = 4 * gi + i2

    erow = lax.broadcasted_iota(I32, (N_EXPERTS, tm), 0)
    hit1 = erow == e1
    hit2 = erow == e2
    onehot = jnp.where(hit1 | hit2, 1.0, 0.0)
    incl = _lane_cumsum(onehot, tm)
    excl = incl - onehot + carry_ref[...]
    rank1 = jnp.sum(jnp.where(hit1, excl, 0.0), axis=0, keepdims=True)
    rank2 = jnp.sum(jnp.where(hit2, excl, 0.0), axis=0, keepdims=True)
    carry_ref[...] = carry_ref[...] + incl[:, tm - 1:tm]

    zi = jnp.zeros((4, tm), I32)
    ids_ref[...] = jnp.concatenate([e1, e2, rank1.astype(I32), rank2.astype(I32), zi], axis=0)
    wts_ref[...] = jnp.concatenate([p1 / tot, p2 / tot, jnp.zeros((6, tm), F32)], axis=0)
    cnt_ref[...] = jnp.broadcast_to(carry_ref[...], cnt_ref.shape)


def _router(h, rw_t, bias_col, tm=2048):
    n = h.shape[0]
    tm = min(tm, n)
    kern = functools.partial(_router_kernel, tm=tm)
    return pl.pallas_call(
        kern,
        grid=(n // tm,),
        in_specs=[pl.BlockSpec((tm, D_MODEL), lambda i: (i, 0)),
                  pl.BlockSpec((N_EXPERTS, D_MODEL), lambda i: (0, 0)),
                  pl.BlockSpec((N_EXPERTS, 1), lambda i: (0, 0))],
        out_specs=[pl.BlockSpec((8, tm), lambda i: (0, i)),
                   pl.BlockSpec((8, tm), lambda i: (0, i)),
                   pl.BlockSpec((N_EXPERTS, HEAD), lambda i: (0, 0))],
        out_shape=[jax.ShapeDtypeStruct((8, n), I32), jax.ShapeDtypeStruct((8, n), F32),
                   jax.ShapeDtypeStruct((N_EXPERTS, HEAD), F32)],
        scratch_shapes=[pltpu.VMEM((N_EXPERTS, 1), F32)],
        compiler_params=_params("arbitrary"),
        name="router",
    )(h, rw_t, bias_col)


def _row_copy(src_hbm, row, dst_ref, slot, sem):
    return pltpu.make_async_copy(src_hbm.at[pl.ds(row, 1), :], dst_ref.at[pl.ds(slot, 1), :], sem)


def _expert_kernel(texp_ref, nused_ref, src_ref, h_hbm, w1_ref, w2_ref, o_ref, x_ref, sem, *, tm):
    i = pl.program_id(0)
    n_used = nused_ref[0]
    slot = i % 2

    def start_rows(tile, buf):
        def body(r, carry):
            _row_copy(h_hbm, src_ref[tile * tm + r], x_ref.at[buf], r, sem.at[buf]).start()
            return carry

        lax.fori_loop(0, tm, body, 0, unroll=8)

    def wait_rows(buf):
        pltpu.make_async_copy(h_hbm.at[pl.ds(0, tm), :], x_ref.at[buf], sem.at[buf]).wait()

    @pl.when(i == 0)
    def _():
        start_rows(0, 0)

    @pl.when(i + 1 < n_used)
    def _():
        start_rows(i + 1, 1 - slot)

    @pl.when(i < n_used)
    def _():
        wait_rows(slot)
        mid = _dot(x_ref[slot].astype(BF16), w1_ref[0])
        gate = mid[:, :D_FF]
        act = (gate * _sigmoid(gate)) * mid[:, D_FF:]
        o_ref[...] = _dot(act.astype(BF16), w2_ref[0])

    @pl.when(i >= n_used)
    def _():
        o_ref[...] = jnp.zeros(o_ref.shape, F32)


def _experts(h, w1, w2, tile_expert, n_used, src_rows, n_tiles):
    tm = MOE_TILE
    kern = functools.partial(_expert_kernel, tm=tm)
    grid_spec = pltpu.PrefetchScalarGridSpec(
        num_scalar_prefetch=3,
        grid=(n_tiles,),
        in_specs=[pl.BlockSpec(memory_space=pl.ANY),
                  pl.BlockSpec((1, D_MODEL, 2 * D_FF), lambda i, te, nu, sr: (te[i], 0, 0)),
                  pl.BlockSpec((1, D_FF, D_MODEL), lambda i, te, nu, sr: (te[i], 0, 0))],
        out_specs=pl.BlockSpec((tm, D_MODEL), lambda i, te, nu, sr: (i, 0)),
        scratch_shapes=[pltpu.VMEM((2, tm, D_MODEL), F32), pltpu.SemaphoreType.DMA((2,))],
    )
    return pl.pallas_call(
        kern,
        grid_spec=grid_spec,
        out_shape=jax.ShapeDtypeStruct((n_tiles * tm, D_MODEL), F32),
        compiler_params=_params("arbitrary"),
        name="experts",
    )(tile_expert, n_used, src_rows, h, w1, w2)


def _combine_kernel(p1_ref, p2_ref, y_hbm, h_ref, w1_ref, w2_ref, g_ref, b_ref, o_ref,
                    a_ref, c_ref, sem_a, sem_c, *, tc, alpha):
    i = pl.program_id(0)
    slot = i % 2

    def start_rows(step, buf):
        def body(r, carry):
            _row_copy(y_hbm, p1_ref[step * tc + r], a_ref.at[buf], r, sem_a.at[buf]).start()
            _row_copy(y_hbm, p2_ref[step * tc + r], c_ref.at[buf], r, sem_c.at[buf]).start()
            return carry

        lax.fori_loop(0, tc, body, 0, unroll=8)

    @pl.when(i == 0)
    def _():
        start_rows(0, 0)

    @pl.when(i + 1 < pl.num_programs(0))
    def _():
        start_rows(i + 1, 1 - slot)

    pltpu.make_async_copy(y_hbm.at[pl.ds(0, tc), :], a_ref.at[slot], sem_a.at[slot]).wait()
    pltpu.make_async_copy(y_hbm.at[pl.ds(0, tc), :], c_ref.at[slot], sem_c.at[slot]).wait()
    z = alpha * h_ref[...] + (w1_ref[...] * a_ref[slot] + w2_ref[...] * c_ref[slot])
    o_ref[...] = _layer_norm_rows(z, g_ref[...], b_ref[...])


def _combine(y, h, pos1, pos2, w1c, w2c, g, b, alpha, tc=256):
    n = h.shape[0]
    kern = functools.partial(_combine_kernel, tc=tc, alpha=alpha)
    row = lambda i, p1, p2: (i, 0)
    fixed = lambda i, p1, p2: (0, 0)
    grid_spec = pltpu.PrefetchScalarGridSpec(
        num_scalar_prefetch=2,
        grid=(n // tc,),
        in_specs=[pl.BlockSpec(memory_space=pl.ANY),
                  pl.BlockSpec((tc, D_MODEL), row),
                  pl.BlockSpec((tc, 1), row), pl.BlockSpec((tc, 1), row),
                  pl.BlockSpec((1, D_MODEL), fixed), pl.BlockSpec((1, D_MODEL), fixed)],
        out_specs=pl.BlockSpec((tc, D_MODEL), row),
        scratch_shapes=[pltpu.VMEM((2, tc, D_MODEL), F32), pltpu.VMEM((2, tc, D_MODEL), F32),
                        pltpu.SemaphoreType.DMA((2,)), pltpu.SemaphoreType.DMA((2,))],
    )
    return pl.pallas_call(
        kern,
        grid_spec=grid_spec,
        out_shape=jax.ShapeDtypeStruct((n, D_MODEL), F32),
        compiler_params=_params("arbitrary"),
        name="moe_combine",
    )(pos1, pos2, y, h, w1c, w2c, g, b)


def _moe(h, rw_t, bias_col, w1, w2, g, b, alpha):
    n = h.shape[0]
    tm = MOE_TILE
    n_tiles = (2 * n) // tm + N_EXPERTS
    ids, wts, cnt = _router(h, rw_t, bias_col)
    counts = cnt[:, 0].astype(I32)
    padded = ((counts + tm - 1) // tm) * tm
    ends = jnp.cumsum(padded)
    offs = ends - padded
    pos1 = offs[ids[0]] + ids[2]
    pos2 = offs[ids[1]] + ids[3]
    tok = jnp.arange(n, dtype=I32)
    src_rows = jnp.zeros((n_tiles * tm,), I32).at[jnp.concatenate([pos1, pos2])].set(
        jnp.concatenate([tok, tok]))
    tile_start = jnp.arange(n_tiles, dtype=I32) * tm
    tile_expert = jnp.minimum(jnp.sum(tile_start[:, None] >= ends[None, :], axis=1),
                              N_EXPERTS - 1).astype(I32)
    n_used = (ends[-1:] // tm).astype(I32)
    y = _experts(h, w1, w2, tile_expert, n_used, src_rows, n_tiles)
    return _combine(y, h, pos1, pos2, wts[0].reshape(n, 1), wts[1].reshape(n, 1), g, b, alpha)


def _rotary_tables(t):
    inv = 1.0 / (ROPE_THETA ** (jnp.arange(0, DA_QK_DIM, 2, dtype=F32) / DA_QK_DIM))
    ang = jnp.arange(t, dtype=F32)[:, None] * inv[None, :]
    cos, sin = jnp.cos(ang), jnp.sin(ang)
    return jnp.tile(cos, (1, 4)), jnp.concatenate([-sin, sin, -sin, sin], axis=1)


def _split_w_in(w):
    main = jnp.concatenate([w[:, :3584], w[:, 3592:7688]], axis=1).astype(BF16)
    zero4 = jnp.zeros((w.shape[0], 4), w.dtype)
    gate = jnp.concatenate([w[:, 3584:3588], zero4, w[:, 3588:3592], zero4, w[:, 7688:7704],
                            jnp.zeros((w.shape[0], HEAD - GATE_ROWS), w.dtype)], axis=1).astype(BF16)
    return main, gate


def kernel(x, mem, w_in, da_lambda, da_norm_g, ml_gate_b, ml_norm_g, gd_conv_w, gd_a_log, gd_dt_bias,
           gd_norm_g, w_out, xa_wq, xa_wkv, xa_wo, router_w, router_bias, moe_w_in, moe_w_out,
           ln_g, ln_b):
    bsz, t, d = x.shape
    mem_len = mem.shape[1]
    depth = w_in.shape[0]
    n = bsz * t
    alpha = (2 * depth) ** 0.25
    cos_t, sin_t = _rotary_tables(t)
    rw_t = router_w.T.astype(BF16)
    bias_col = router_bias.reshape(N_EXPERTS, 1).astype(F32)
    mem2 = mem.reshape(bsz * mem_len, d)
    zero4 = jnp.zeros((4,), F32)
    h = x.reshape(n, d)
    for l in range(depth):
        lambda_init = 0.8 - 0.6 * math.exp(-0.3 * l)
        w_main, w_gate = _split_w_in(w_in[l])
        proj = _matmul(h, w_main, F32, 1024, 512)
        gates = _matmul(h, w_gate, F32, 1024, HEAD)
        gt = gates[:, :GATE_ROWS].reshape(bsz, t, GATE_ROWS).transpose(0, 2, 1)
        consts = jnp.concatenate([ml_gate_b[l, 0], zero4, ml_gate_b[l, 1], zero4,
                                  gd_dt_bias[l], gd_a_log[l]]).reshape(GATE_ROWS, 1)
        r_gate, b_cum, r_max, beta, gc = _gate_prep(gt, consts)
        y_da = _diff_attention(proj, cos_t, sin_t, da_lambda[l], da_norm_g[l].reshape(1, HEAD),
                               lambda_init, bsz, t)
        y_ml = _mlstm(proj, b_cum, r_gate, r_max, ml_norm_g[l].reshape(1, HEAD), bsz, t)
        y_gd = _gated_deltanet(proj, gd_conv_w[l], beta, gc, gd_norm_g[l].reshape(1, HEAD), bsz, t)
        wo = w_out[l].astype(BF16)
        h = _proj_ln([y_da, y_ml, y_gd], [wo[:512], wo[512:1024], wo[1024:]], h,
                     ln_g[l, 0].reshape(1, d), ln_b[l, 0].reshape(1, d), alpha)
        q = _matmul(h, xa_wq[l].astype(BF16), BF16, 1024, 512)
        kv = _matmul(mem2, xa_wkv[l].astype(BF16), BF16, 1024, 512)
        xo = _cross_attention(q, kv, bsz, t, mem_len)
        h = _proj_ln([xo], [xa_wo[l].astype(BF16)], h,
                     ln_g[l, 1].reshape(1, d), ln_b[l, 1].reshape(1, d), alpha)
        h = _moe(h, rw_t, bias_col, moe_w_in[l].astype(BF16), moe_w_out[l].astype(BF16),
                 ln_g[l, 2].reshape(1, d), ln_b[l, 2].reshape(1, d), alpha)
    return h.reshape(bsz, t, d)
```

```python
import functools
import math

import jax
import jax.numpy as jnp
from jax import lax
from jax.experimental import pallas as pl
from jax.experimental.pallas import tpu as pltpu

F32 = jnp.float32
BF16 = jnp.bfloat16
I32 = jnp.int32

D_MODEL = 2048
HEAD = 128
DA_HEADS = 4
DA_QK_DIM = 64
ML_HEADS = 4
GD_HEADS = 8
CHUNK = 64
GD_PACK = 4
CONV_W = 4
ROPE_THETA = 10000.0
XA_HEADS = 4
XA_DIM = D_MODEL // XA_HEADS
N_EXPERTS = 16
N_GROUPS = 4
GROUP_SIZE = 4
D_FF = D_MODEL // 2
MAIN_WIDTH = 7680
GATE_ROWS = 32
NEG = -1e30
VMEM_LIMIT = 56 * 1024 * 1024
MOE_TILE = 256
ATT_BLOCK = 512

CB_DA_Q, CB_DA_K, CB_DA_V = 0, 4, 8
CB_ML_Q, CB_ML_K, CB_ML_V, CB_ML_O = 12, 16, 20, 24
CB_GD_Q, CB_GD_K, CB_GD_V, CB_GD_Z = 28, 36, 44, 52


def _params(*sem):
    return pltpu.CompilerParams(dimension_semantics=sem, vmem_limit_bytes=VMEM_LIMIT)


def _dot(a, b):
    return jnp.dot(a, b, preferred_element_type=F32)


def _dot_nt(a, b):
    return lax.dot_general(a, b, (((1,), (1,)), ((), ())), preferred_element_type=F32)


def _dot_tn(a, b):
    return lax.dot_general(a, b, (((0,), (0,)), ((), ())), preferred_element_type=F32)


def _sigmoid(x):
    return 1.0 / (1.0 + jnp.exp(-x))


def _col_from_row(row, n):
    ri = lax.broadcasted_iota(I32, (n, n), 0)
    ci = lax.broadcasted_iota(I32, (n, n), 1)
    return jnp.sum(jnp.where(ri == ci, jnp.broadcast_to(row, (n, n)), 0.0), axis=1, keepdims=True)


def _lane_cumsum(x, seg):
    lane = lax.broadcasted_iota(I32, x.shape, 1) % seg
    s = 1
    while s < seg:
        x = x + jnp.where(lane >= s, pltpu.roll(x, s, 1), 0.0)
        s *= 2
    return x


def _mm_kernel(x_ref, w_ref, o_ref, xb_ref):
    @pl.when(pl.program_id(1) == 0)
    def _():
        xb_ref[...] = x_ref[...].astype(BF16)

    o_ref[...] = _dot(xb_ref[...], w_ref[...]).astype(o_ref.dtype)


def _matmul(x, w, out_dtype, tm, tn):
    m, k = x.shape
    n = w.shape[1]
    tm = min(tm, m)
    return pl.pallas_call(
        _mm_kernel,
        grid=(m // tm, n // tn),
        in_specs=[pl.BlockSpec((tm, k), lambda i, j: (i, 0)),
                  pl.BlockSpec((k, tn), lambda i, j: (0, j))],
        out_specs=pl.BlockSpec((tm, tn), lambda i, j: (i, j)),
        out_shape=jax.ShapeDtypeStruct((m, n), out_dtype),
        scratch_shapes=[pltpu.VMEM((tm, k), BF16)],
        compiler_params=_params("arbitrary", "arbitrary"),
        name="matmul",
    )(x, w)


def _for_blocks(n, fn):
    def pair(jj, carry):
        fn(2 * jj)
        fn(2 * jj + 1)
        return carry

    lax.fori_loop(0, n // 2, pair, 0)

    @pl.when(n % 2 == 1)
    def _():
        fn(n - 1)


def _lane_cummax(x):
    lane = lax.broadcasted_iota(I32, x.shape, 1)
    s = 1
    while s < x.shape[1]:
        x = jnp.maximum(x, jnp.where(lane >= s, pltpu.roll(x, s, 1), NEG))
        s *= 2
    return x


def _gate_kernel(g_ref, c_ref, r_ref, b_ref, rmax_ref, beta_ref, gc_ref):
    g = g_ref[0]
    c = c_ref[...]
    li = g[0:8] + c[0:8]
    xf = g[8:16] + c[8:16]
    lf = jnp.minimum(xf, 0.0) - jnp.log(1.0 + jnp.exp(-jnp.abs(xf)))
    b = _lane_cumsum(lf, lf.shape[1])
    b_ref[0] = b
    r = li - b
    r_ref[0] = r
    rmax_ref[0] = _lane_cummax(r)
    xa = g[16:24] + c[16:24]
    sp = jnp.maximum(xa, 0.0) + jnp.log(1.0 + jnp.exp(-jnp.abs(xa)))
    gdec = -jnp.exp(c[24:32]) * sp
    gc_ref[0] = _lane_cumsum(gdec, CHUNK)
    beta_ref[0] = _sigmoid(g[24:32])


def _gate_prep(gt, consts):
    bsz, _, t = gt.shape
    out = jax.ShapeDtypeStruct((bsz, 8, t), F32)
    spec = pl.BlockSpec((1, 8, t), lambda b: (b, 0, 0))
    return pl.pallas_call(
        _gate_kernel,
        grid=(bsz,),
        in_specs=[pl.BlockSpec((1, GATE_ROWS, t), lambda b: (b, 0, 0)),
                  pl.BlockSpec((GATE_ROWS, 1), lambda b: (0, 0))],
        out_specs=[spec] * 5,
        out_shape=[out] * 5,
        compiler_params=_params("arbitrary"),
        name="gate_prep",
    )(gt, consts)


def _rotary(t, c, s):
    lane = lax.broadcasted_iota(I32, t.shape, 1)
    swapped = jnp.where((lane % DA_QK_DIM) < DA_QK_DIM // 2,
                        pltpu.roll(t, HEAD - DA_QK_DIM // 2, 1), pltpu.roll(t, DA_QK_DIM // 2, 1))
    return t * c + swapped * s


def _da_kernel(q_ref, k_ref, v_ref, cq_ref, sq_ref, ck_ref, sk_ref, lam_ref, g_ref, o_ref,
               kr_ref, vb_ref, s_ref, mx_ref, ls_ref, acc_ref, *, bq, lambda_init):
    qi = pl.program_id(2)
    nh = bq // HEAD

    @pl.when(qi == 0)
    def _():
        kr_ref[...] = _rotary(k_ref[...], ck_ref[...], sk_ref[...]).astype(BF16)
        vb_ref[...] = v_ref[...].astype(BF16)

    qr = _rotary(q_ref[...], cq_ref[...], sq_ref[...]) * (DA_QK_DIM ** -0.5)
    lane = lax.broadcasted_iota(I32, qr.shape, 1)
    qs = jnp.concatenate([jnp.where(lane < DA_QK_DIM, qr, 0.0),
                          jnp.where(lane >= DA_QK_DIM, qr, 0.0)], axis=0).astype(BF16)

    def fold(x, op):
        out = x[:, :HEAD]
        for i in range(1, nh):
            out = op(out, x[:, i * HEAD:(i + 1) * HEAD])
        return out

    mx_ref[...] = jnp.full(mx_ref.shape, NEG, F32)

    def scores(j):
        off = pl.multiple_of(j * bq, bq)
        s = _dot_nt(qs, kr_ref[pl.ds(off, bq), :])
        s_ref[j] = s
        mx_ref[...] = jnp.maximum(mx_ref[...], fold(s, jnp.maximum))

    _for_blocks(qi, scores)
    offd = pl.multiple_of(qi * bq, bq)
    sd = _dot_nt(qs, kr_ref[pl.ds(offd, bq), :])
    r = lax.broadcasted_iota(I32, sd.shape, 0)
    c = lax.broadcasted_iota(I32, sd.shape, 1)
    sd = jnp.where(jnp.where(r >= bq, r - bq, r) >= c, sd, NEG)
    m = jnp.max(jnp.maximum(mx_ref[...], fold(sd, jnp.maximum)), axis=1, keepdims=True)
    mb = jnp.broadcast_to(m, (2 * bq, HEAD))

    ls_ref[...] = jnp.zeros(ls_ref.shape, F32)
    acc_ref[...] = jnp.zeros(acc_ref.shape, F32)

    def accumulate(s, off):
        ps = [jnp.exp(s[:, i * HEAD:(i + 1) * HEAD] - mb) for i in range(nh)]
        tot = ps[0]
        for p in ps[1:]:
            tot = tot + p
        ls_ref[...] += tot
        pb = jnp.concatenate([p.astype(BF16) for p in ps], axis=1)
        acc_ref[...] += _dot(pb, vb_ref[pl.ds(off, bq), :])

    _for_blocks(qi, lambda j: accumulate(s_ref[j], pl.multiple_of(j * bq, bq)))
    accumulate(sd, offd)

    lam = lam_ref[...]
    s1 = jnp.sum(lam[0:1] * lam[1:2], axis=1, keepdims=True)
    s2 = jnp.sum(lam[2:3] * lam[3:4], axis=1, keepdims=True)
    lam_full = jnp.exp(s1) - jnp.exp(s2) + lambda_init
    acc = acc_ref[...]
    l = jnp.sum(ls_ref[...], axis=1, keepdims=True)
    o = acc[:bq] / l[:bq] - lam_full * (acc[bq:] / l[bq:])
    o = o * lax.rsqrt(jnp.mean(o * o, axis=1, keepdims=True) + 1e-6) * g_ref[...]
    o_ref[...] = (o * (1.0 - lambda_init)).astype(o_ref.dtype)


def _diff_attention(proj, cos_t, sin_t, lam, norm_g, lambda_init, bsz, t):
    bq = ATT_BLOCK
    nq = t // bq
    kern = functools.partial(_da_kernel, bq=bq, lambda_init=lambda_init)
    tab_q = pl.BlockSpec((bq, HEAD), lambda b, h, i: (i, 0))
    tab_k = pl.BlockSpec((t, HEAD), lambda b, h, i: (0, 0))
    return pl.pallas_call(
        kern,
        grid=(bsz, DA_HEADS, nq),
        in_specs=[pl.BlockSpec((bq, HEAD), lambda b, h, i: (b * nq + i, CB_DA_Q + h)),
                  pl.BlockSpec((t, HEAD), lambda b, h, i: (b, CB_DA_K + h)),
                  pl.BlockSpec((t, HEAD), lambda b, h, i: (b, CB_DA_V + h)),
                  tab_q, tab_q, tab_k, tab_k,
                  pl.BlockSpec((4, DA_QK_DIM), lambda b, h, i: (0, 0)),
                  pl.BlockSpec((1, HEAD), lambda b, h, i: (0, 0))],
        out_specs=pl.BlockSpec((bq, HEAD), lambda b, h, i: (b * nq + i, h)),
        out_shape=jax.ShapeDtypeStruct((bsz * t, DA_HEADS * HEAD), BF16),
        scratch_shapes=[pltpu.VMEM((t, HEAD), BF16), pltpu.VMEM((t, HEAD), BF16),
                        pltpu.VMEM((nq, 2 * bq, bq), F32),
                        pltpu.VMEM((2 * bq, HEAD), F32), pltpu.VMEM((2 * bq, HEAD), F32),
                        pltpu.VMEM((2 * bq, HEAD), F32)],
        compiler_params=_params("arbitrary", "arbitrary", "arbitrary"),
        name="diff_attention",
    )(proj, proj, proj, cos_t, sin_t, cos_t, sin_t, lam, norm_g)


def _ml_kernel(q_ref, k_ref, v_ref, op_ref, b_ref, r_ref, rmax_ref, g_ref, o_ref,
               kb_ref, vb_ref, den_ref, num_ref, *, bq):
    qi = pl.program_id(2)
    nh = bq // HEAD

    @pl.when(qi == 0)
    def _():
        kb_ref[...] = (k_ref[...] * (HEAD ** -0.5)).astype(BF16)
        vb_ref[...] = v_ref[...].astype(BF16)

    qb = q_ref[...].astype(BF16)
    b_col = _col_from_row(b_ref[0, 0, pl.ds(qi, 1), :], bq)
    rmax_col = _col_from_row(rmax_ref[0, 0, pl.ds(qi, 1), :], bq)
    den_ref[...] = jnp.zeros(den_ref.shape, F32)
    num_ref[...] = jnp.zeros(num_ref.shape, F32)

    def block(j, masked):
        off = pl.multiple_of(j * bq, bq)
        s = _dot_nt(qb, kb_ref[pl.ds(off, bq), :])
        logw = r_ref[0, 0, pl.ds(j, 1), :] - rmax_col
        if masked:
            r = lax.broadcasted_iota(I32, s.shape, 0)
            c = lax.broadcasted_iota(I32, s.shape, 1)
            logw = jnp.where(r >= c, logw, NEG)
        sw = s * jnp.exp(logw)
        tot = sw[:, :HEAD]
        for i in range(1, nh):
            tot = tot + sw[:, i * HEAD:(i + 1) * HEAD]
        den_ref[...] += tot
        num_ref[...] += _dot(sw.astype(BF16), vb_ref[pl.ds(off, bq), :])

    _for_blocks(qi, lambda j: block(j, False))
    block(qi, True)

    den = jnp.sum(den_ref[...], axis=1, keepdims=True)
    h = num_ref[...] / jnp.maximum(jnp.abs(den), jnp.exp(-(b_col + rmax_col)))
    h = h * lax.rsqrt(jnp.mean(h * h, axis=1, keepdims=True) + 1e-6) * g_ref[...]
    o_ref[...] = (_sigmoid(op_ref[...]) * h).astype(o_ref.dtype)


def _mlstm(proj, b_cum, r_gate, r_max, norm_g, bsz, t):
    bq = ATT_BLOCK
    nq = t // bq
    kern = functools.partial(_ml_kernel, bq=bq)
    gate = pl.BlockSpec((1, 1, nq, bq), lambda b, h, i: (b, h, 0, 0))
    return pl.pallas_call(
        kern,
        grid=(bsz, ML_HEADS, nq),
        in_specs=[pl.BlockSpec((bq, HEAD), lambda b, h, i: (b * nq + i, CB_ML_Q + h)),
                  pl.BlockSpec((t, HEAD), lambda b, h, i: (b, CB_ML_K + h)),
                  pl.BlockSpec((t, HEAD), lambda b, h, i: (b, CB_ML_V + h)),
                  pl.BlockSpec((bq, HEAD), lambda b, h, i: (b * nq + i, CB_ML_O + h)),
                  gate, gate, gate,
                  pl.BlockSpec((1, HEAD), lambda b, h, i: (0, 0))],
        out_specs=pl.BlockSpec((bq, HEAD), lambda b, h, i: (b * nq + i, h)),
        out_shape=jax.ShapeDtypeStruct((bsz * t, ML_HEADS * HEAD), BF16),
        scratch_shapes=[pltpu.VMEM((t, HEAD), BF16), pltpu.VMEM((t, HEAD), BF16),
                        pltpu.VMEM((bq, HEAD), F32), pltpu.VMEM((bq, HEAD), F32)],
        compiler_params=_params("arbitrary", "arbitrary", "arbitrary"),
        name="mlstm",
    )(proj, proj, proj, proj, b_cum.reshape(bsz, 8, nq, bq), r_gate.reshape(bsz, 8, nq, bq),
      r_max.reshape(bsz, 8, nq, bq), norm_g)


def _gd_kernel(q_ref, k_ref, v_ref, z_ref, cwq_ref, cwk_ref, cwv_ref, beta_ref, gc_ref, g_ref, o_ref,
               mq_ref, no_ref, egl_ref, *, t, heads):
    L = CHUNK
    S2 = GD_PACK * L
    nc = t // L
    ri = lax.broadcasted_iota(I32, (S2, S2), 0)
    ci = lax.broadcasted_iota(I32, (S2, S2), 1)
    same = (ri // L) == (ci // L)
    tri = same & (ri >= ci)
    strict = same & (ri > ci)
    eye = (ri == ci).astype(F32)
    row_chunk = lax.broadcasted_iota(I32, (S2, 1), 0) // L

    def conv_silu(x_ref, cw_ref, lo, sc, base):
        cur = x_ref[pl.ds(base, S2), lo:lo + HEAD]
        prev = x_ref[pl.ds(pl.multiple_of(jnp.maximum(base - 8, 0), 8), 8), lo:lo + HEAD]
        blk = jnp.concatenate([jnp.where(sc > 0, prev, 0.0), cur], axis=0)
        cw = cw_ref[:, lo:lo + HEAD]
        y = blk[5:5 + S2] * cw[0:1]
        for i in range(1, CONV_W):
            y = y + blk[5 + i:5 + i + S2] * cw[i:i + 1]
        return y * _sigmoid(y)

    def mm(a, b):
        return _dot(a.astype(BF16), b.astype(BF16))

    def local_head(hh, sc):
        lo = hh * HEAD
        base = pl.multiple_of(sc * S2, S2)
        q = conv_silu(q_ref, cwq_ref, lo, sc, base)
        k = conv_silu(k_ref, cwk_ref, lo, sc, base)
        v = conv_silu(v_ref, cwv_ref, lo, sc, base)
        q = q * lax.rsqrt(jnp.sum(q * q, axis=1, keepdims=True) + 1e-6) * (HEAD ** -0.5)
        k = k * lax.rsqrt(jnp.sum(k * k, axis=1, keepdims=True) + 1e-6)
        gc_row = gc_ref[0, hh, pl.ds(sc, 1), :]
        gc_col = _col_from_row(gc_row, S2)
        beta_col = _col_from_row(beta_ref[0, hh, pl.ds(sc, 1), :], S2)
        decay = jnp.exp(jnp.where(tri, gc_col - gc_row, NEG))
        kb = k * beta_col
        kq = _dot_nt(jnp.concatenate([kb, q], axis=0).astype(BF16), k.astype(BF16))
        a = jnp.where(strict, kq[:S2] * decay, 0.0)
        attn = jnp.where(tri, kq[S2:] * decay, 0.0).astype(BF16)
        x = eye - a
        p = mm(a, a)
        for _ in range(4):
            px = mm(jnp.concatenate([p, x], axis=0), p)
            p = px[:S2]
            x = x + px[S2:]
        x = x + mm(x, p)
        rhs = jnp.concatenate([kb * jnp.exp(gc_col), v * beta_col], axis=1)
        wu = mm(x, rhs)
        gls = [gc_row[:, c * L + L - 1:c * L + L] for c in range(GD_PACK)]
        gl_col = gls[-1]
        for c in reversed(range(GD_PACK - 1)):
            gl_col = jnp.where(row_chunk == c, gls[c], gl_col)
        kd = (k * jnp.exp(gl_col - gc_col)).astype(BF16)
        aw = _dot(attn, wu.astype(BF16))
        qprime = (q * jnp.exp(gc_col) - aw[:, :HEAD]).astype(BF16)
        wbd = jnp.concatenate([jnp.where(row_chunk == c, wu, 0.0).astype(BF16) for c in range(GD_PACK)],
                              axis=1)
        mn = _dot_tn(kd, wbd)
        for c in range(GD_PACK):
            rows = slice(c * L, (c + 1) * L)
            blk = pl.multiple_of((GD_PACK * sc + c) * (HEAD + L), L)
            mq_ref[hh, pl.ds(blk, HEAD), :] = mn[:, 2 * c * HEAD:(2 * c + 1) * HEAD].astype(BF16)
            mq_ref[hh, pl.ds(blk + HEAD, L), :] = qprime[rows]
            no_ref[hh, pl.ds(blk, HEAD), :] = mn[:, (2 * c + 1) * HEAD:(2 * c + 2) * HEAD]
            no_ref[hh, pl.ds(blk + HEAD, L), :] = aw[rows, HEAD:]
            egl_ref[hh, pl.ds(GD_PACK * sc + c, 1), :] = jnp.broadcast_to(jnp.exp(gls[c]), (1, HEAD))

    def local(sc, carry):
        for hh in range(heads):
            local_head(hh, sc)
        return carry

    lax.fori_loop(0, nc // GD_PACK, local, 0, unroll=2)

    gn = g_ref[...]

    def step(c, states):
        base = pl.multiple_of(c * L, L)
        blk = pl.multiple_of(c * (HEAD + L), L)
        new = []
        for hh in range(heads):
            lo = hh * HEAD
            s = states[hh]
            ms = _dot(mq_ref[hh, pl.ds(blk, HEAD + L), :], s.astype(BF16))
            new.append(s * egl_ref[hh, pl.ds(c, 1), :] - ms[:HEAD] + no_ref[hh, pl.ds(blk, HEAD), :])
            o = ms[HEAD:] + no_ref[hh, pl.ds(blk + HEAD, L), :]
            o = o * lax.rsqrt(jnp.mean(o * o, axis=1, keepdims=True) + 1e-6) * gn
            z = z_ref[pl.ds(base, L), lo:lo + HEAD]
            o_ref[pl.ds(base, L), lo:lo + HEAD] = (o * (z * _sigmoid(z))).astype(o_ref.dtype)
        return tuple(new)

    lax.fori_loop(0, nc, step, tuple(jnp.zeros((HEAD, HEAD), F32) for _ in range(heads)), unroll=2)


def _gated_deltanet(proj, conv_w, beta, gc, norm_g, bsz, t, heads=2):
    nc = t // CHUNK
    width = heads * HEAD
    kern = functools.partial(_gd_kernel, t=t, heads=heads)

    def slab(cb):
        return pl.BlockSpec((t, width), lambda b, h: (b, cb // heads + h))

    def cw(off):
        return pl.BlockSpec((CONV_W, width), lambda b, h: (0, off // heads + h))

    nsc, srows = nc // GD_PACK, GD_PACK * CHUNK
    gate = pl.BlockSpec((1, heads, nsc, srows), lambda b, h: (b, h, 0, 0))
    return pl.pallas_call(
        kern,
        grid=(bsz, GD_HEADS // heads),
        in_specs=[slab(CB_GD_Q), slab(CB_GD_K), slab(CB_GD_V), slab(CB_GD_Z),
                  cw(0), cw(GD_HEADS), cw(2 * GD_HEADS), gate, gate,
                  pl.BlockSpec((1, HEAD), lambda b, h: (0, 0))],
        out_specs=pl.BlockSpec((t, width), lambda b, h: (b, h)),
        out_shape=jax.ShapeDtypeStruct((bsz * t, GD_HEADS * HEAD), BF16),
        scratch_shapes=[pltpu.VMEM((heads, nc * (HEAD + CHUNK), HEAD), BF16),
                        pltpu.VMEM((heads, nc * (HEAD + CHUNK), HEAD), F32),
                        pltpu.VMEM((heads, nc, HEAD), F32)],
        compiler_params=_params("arbitrary", "arbitrary"),
        name="gated_deltanet",
    )(proj, proj, proj, proj, conv_w, conv_w, conv_w,
      beta.reshape(bsz, 8, nsc, srows), gc.reshape(bsz, 8, nsc, srows), norm_g)


def _layer_norm_rows(z, g, b):
    mu = jnp.mean(z, axis=1, keepdims=True)
    zc = z - mu
    var = jnp.mean(zc * zc, axis=1, keepdims=True)
    return zc * lax.rsqrt(var + 1e-5) * g + b


def _proj_ln_kernel(*refs, nx, alpha, tn):
    x_refs = refs[:nx]
    w_refs = refs[nx:2 * nx]
    h_ref, g_ref, b_ref, o_ref, z_ref = refs[2 * nx:]
    for n in range(0, D_MODEL, tn):
        acc = alpha * h_ref[:, n:n + tn]
        for x_ref, w_ref in zip(x_refs, w_refs):
            acc = acc + _dot(x_ref[...], w_ref[:, n:n + tn])
        z_ref[:, n:n + tn] = acc
    o_ref[...] = _layer_norm_rows(z_ref[...], g_ref[...], b_ref[...])


def _proj_ln(xs, ws, h, g, b, alpha, tm=256):
    n_rows = h.shape[0]
    nx = len(xs)
    kern = functools.partial(_proj_ln_kernel, nx=nx, alpha=alpha, tn=512)
    row = lambda i: (i, 0)
    fixed = lambda i: (0, 0)
    in_specs = ([pl.BlockSpec((tm, x.shape[1]), row) for x in xs]
                + [pl.BlockSpec(w.shape, fixed) for w in ws]
                + [pl.BlockSpec((tm, D_MODEL), row),
                   pl.BlockSpec((1, D_MODEL), fixed), pl.BlockSpec((1, D_MODEL), fixed)])
    return pl.pallas_call(
        kern,
        grid=(n_rows // tm,),
        in_specs=in_specs,
        out_specs=pl.BlockSpec((tm, D_MODEL), row),
        out_shape=jax.ShapeDtypeStruct((n_rows, D_MODEL), F32),
        scratch_shapes=[pltpu.VMEM((tm, D_MODEL), F32)],
        compiler_params=_params("arbitrary"),
        name="proj_ln",
    )(*xs, *ws, h, g, b)


def _xa_kernel(q_ref, kv_ref, o_ref):
    for hh in range(XA_HEADS):
        lo = hh * XA_DIM
        s = _dot_nt(q_ref[:, lo:lo + XA_DIM], kv_ref[:, lo:lo + XA_DIM]) * (XA_DIM ** -0.5)
        s = s - jnp.max(s, axis=1, keepdims=True)
        p = jnp.exp(s)
        p = p / jnp.sum(p, axis=1, keepdims=True)
        o = _dot(p.astype(BF16), kv_ref[:, D_MODEL + lo:D_MODEL + lo + XA_DIM])
        o_ref[:, lo:lo + XA_DIM] = o.astype(o_ref.dtype)


def _cross_attention(q, kv, bsz, t, mem_len, tq=512):
    nq = t // tq
    return pl.pallas_call(
        _xa_kernel,
        grid=(bsz, nq),
        in_specs=[pl.BlockSpec((tq, D_MODEL), lambda b, i: (b * nq + i, 0)),
                  pl.BlockSpec((mem_len, 2 * D_MODEL), lambda b, i: (b, 0))],
        out_specs=pl.BlockSpec((tq, D_MODEL), lambda b, i: (b * nq + i, 0)),
        out_shape=jax.ShapeDtypeStruct((bsz * t, D_MODEL), BF16),
        compiler_params=_params("arbitrary", "arbitrary"),
        name="cross_attention",
    )(q, kv)


def _router_kernel(x_ref, rw_ref, bias_ref, ids_ref, wts_ref, cnt_ref, carry_ref, *, tm):
    i = pl.program_id(0)

    @pl.when(i == 0)
    def _():
        carry_ref[...] = jnp.zeros(carry_ref.shape, F32)

    logits = _dot_nt(rw_ref[...], x_ref[...].astype(BF16))
    scores = _sigmoid(logits)
    sel = scores + bias_ref[...]
    srow = [sel[e:e + 1] for e in range(N_EXPERTS)]
    prow = [scores[e:e + 1] for e in range(N_EXPERTS)]

    gscore = []
    for g in range(N_GROUPS):
        s0, s1, s2, s3 = srow[4 * g:4 * g + 4]
        a, b = jnp.maximum(s0, s1), jnp.minimum(s0, s1)
        c, d = jnp.maximum(s2, s3), jnp.minimum(s2, s3)
        gscore.append(jnp.maximum(a, c) + jnp.maximum(jnp.minimum(a, c), jnp.maximum(b, d)))
    gmax = jnp.maximum(jnp.maximum(gscore[0], gscore[1]), jnp.maximum(gscore[2], gscore[3]))
    gi = jnp.where(gscore[0] == gmax, 0, jnp.where(gscore[1] == gmax, 1,
                                                    jnp.where(gscore[2] == gmax, 2, 3)))

    def pick(rows, j):
        return jnp.where(gi == 0, rows[j], jnp.where(gi == 1, rows[4 + j],
                                                      jnp.where(gi == 2, rows[8 + j], rows[12 + j])))

    v = [pick(srow, j) for j in range(GROUP_SIZE)]
    sc = [pick(prow, j) for j in range(GROUP_SIZE)]
    v1 = jnp.maximum(jnp.maximum(v[0], v[1]), jnp.maximum(v[2], v[3]))
    i1 = jnp.where(v[0] == v1, 0, jnp.where(v[1] == v1, 1, jnp.where(v[2] == v1, 2, 3)))
    rest = [jnp.where(i1 == j, -jnp.inf, v[j]) for j in range(GROUP_SIZE)]
    v2 = jnp.maximum(jnp.maximum(rest[0], rest[1]), jnp.maximum(rest[2], rest[3]))
    i2 = jnp.where(rest[0] == v2, 0, jnp.where(rest[1] == v2, 1, jnp.where(rest[2] == v2, 2, 3)))

    def pick_idx(idx):
        return jnp.where(idx == 0, sc[0], jnp.where(idx == 1, sc[1], jnp.where(idx == 2, sc[2], sc[3])))

    p1, p2 = pick_idx(i1), pick_idx(i2)
    tot = p1 + p2
    e1 = 4 * gi + i1
    e2 = 4 * gi + i2

    erow = lax.broadcasted_iota(I32, (N_EXPERTS, tm), 0)
    hit1 = erow == e1
    hit2 = erow == e2
    onehot = jnp.where(hit1 | hit2, 1.0, 0.0)
    incl = _lane_cumsum(onehot, tm)
    excl = incl - onehot + carry_ref[...]
    rank1 = jnp.sum(jnp.where(hit1, excl, 0.0), axis=0, keepdims=True)
    rank2 = jnp.sum(jnp.where(hit2, excl, 0.0), axis=0, keepdims=True)
    carry_ref[...] = carry_ref[...] + incl[:, tm - 1:tm]

    zi = jnp.zeros((4, tm), I32)
    ids_ref[...] = jnp.concatenate([e1, e2, rank1.astype(I32), rank2.astype(I32), zi], axis=0)
    wts_ref[...] = jnp.concatenate([p1 / tot, p2 / tot, jnp.zeros((6, tm), F32)], axis=0)
    cnt_ref[...] = jnp.broadcast_to(carry_ref[...], cnt_ref.shape)


def _router(h, rw_t, bias_col, tm=2048):
    n = h.shape[0]
    tm = min(tm, n)
    kern = functools.partial(_router_kernel, tm=tm)
    return pl.pallas_call(
        kern,
        grid=(n // tm,),
        in_specs=[pl.BlockSpec((tm, D_MODEL), lambda i: (i, 0)),
                  pl.BlockSpec((N_EXPERTS, D_MODEL), lambda i: (0, 0)),
                  pl.BlockSpec((N_EXPERTS, 1), lambda i: (0, 0))],
        out_specs=[pl.BlockSpec((8, tm), lambda i: (0, i)),
                   pl.BlockSpec((8, tm), lambda i: (0, i)),
                   pl.BlockSpec((N_EXPERTS, HEAD), lambda i: (0, 0))],
        out_shape=[jax.ShapeDtypeStruct((8, n), I32), jax.ShapeDtypeStruct((8, n), F32),
                   jax.ShapeDtypeStruct((N_EXPERTS, HEAD), F32)],
        scratch_shapes=[pltpu.VMEM((N_EXPERTS, 1), F32)],
        compiler_params=_params("arbitrary"),
        name="router",
    )(h, rw_t, bias_col)


def _row_copy(src_hbm, row, dst_ref, slot, sem):
    return pltpu.make_async_copy(src_hbm.at[pl.ds(row, 1), :], dst_ref.at[pl.ds(slot, 1), :], sem)


def _expert_kernel(texp_ref, nused_ref, src_ref, h_hbm, w1_ref, w2_ref, o_ref, xa_ref, xb_ref, sem, *, tm):
    i = pl.program_id(0)
    n_used = nused_ref[0]
    bufs = (xa_ref, xb_ref)

    def wait_rows(buf):
        pltpu.make_async_copy(h_hbm.at[pl.ds(0, tm), :], bufs[buf], sem.at[buf]).wait()

    @pl.when(i == 0)
    def _():
        def body(r, carry):
            _row_copy(h_hbm, src_ref[r], xa_ref, r, sem.at[0]).start()
            return carry

        lax.fori_loop(0, tm, body, 0, unroll=8)

    def tile(cur):
        wait_rows(cur)
        nxt = jnp.minimum(i + 1, n_used - 1) * tm
        for r in range(tm):
            _row_copy(h_hbm, src_ref[nxt + r], bufs[1 - cur], r, sem.at[1 - cur]).start()
        mid = _dot(bufs[cur][...].astype(BF16), w1_ref[0])
        gate = mid[:, :D_FF]
        act = (gate * _sigmoid(gate)) * mid[:, D_FF:]
        o_ref[...] = _dot(act.astype(BF16), w2_ref[0])

        @pl.when(i + 1 == n_used)
        def _():
            wait_rows(1 - cur)

    @pl.when((i < n_used) & (i % 2 == 0))
    def _():
        tile(0)

    @pl.when((i < n_used) & (i % 2 == 1))
    def _():
        tile(1)

    @pl.when(i >= n_used)
    def _():
        o_ref[...] = jnp.zeros(o_ref.shape, F32)


def _experts(h, w1, w2, tile_expert, n_used, src_rows, n_tiles):
    tm = MOE_TILE
    kern = functools.partial(_expert_kernel, tm=tm)
    grid_spec = pltpu.PrefetchScalarGridSpec(
        num_scalar_prefetch=3,
        grid=(n_tiles,),
        in_specs=[pl.BlockSpec(memory_space=pl.ANY),
                  pl.BlockSpec((1, D_MODEL, 2 * D_FF), lambda i, te, nu, sr: (te[i], 0, 0)),
                  pl.BlockSpec((1, D_FF, D_MODEL), lambda i, te, nu, sr: (te[i], 0, 0))],
        out_specs=pl.BlockSpec((tm, D_MODEL), lambda i, te, nu, sr: (i, 0)),
        scratch_shapes=[pltpu.VMEM((tm, D_MODEL), F32), pltpu.VMEM((tm, D_MODEL), F32),
                        pltpu.SemaphoreType.DMA((2,))],
    )
    return pl.pallas_call(
        kern,
        grid_spec=grid_spec,
        out_shape=jax.ShapeDtypeStruct((n_tiles * tm, D_MODEL), F32),
        compiler_params=_params("arbitrary"),
        name="experts",
    )(tile_expert, n_used, src_rows, h, w1, w2)


def _combine_kernel(p1_ref, p2_ref, y_hbm, h_ref, w1_ref, w2_ref, g_ref, b_ref, o_ref,
                    a0_ref, a1_ref, c0_ref, c1_ref, sem_a, sem_c, *, tc, alpha):
    i = pl.program_id(0)
    last = pl.num_programs(0) - 1
    a_bufs = (a0_ref, a1_ref)
    c_bufs = (c0_ref, c1_ref)

    def wait_rows(buf):
        pltpu.make_async_copy(y_hbm.at[pl.ds(0, tc), :], a_bufs[buf], sem_a.at[buf]).wait()
        pltpu.make_async_copy(y_hbm.at[pl.ds(0, tc), :], c_bufs[buf], sem_c.at[buf]).wait()

    @pl.when(i == 0)
    def _():
        def body(r, carry):
            _row_copy(y_hbm, p1_ref[r], a0_ref, r, sem_a.at[0]).start()
            _row_copy(y_hbm, p2_ref[r], c0_ref, r, sem_c.at[0]).start()
            return carry

        lax.fori_loop(0, tc, body, 0, unroll=8)

    def step(cur):
        wait_rows(cur)
        nxt = jnp.minimum(i + 1, last) * tc
        for r in range(tc):
            _row_copy(y_hbm, p1_ref[nxt + r], a_bufs[1 - cur], r, sem_a.at[1 - cur]).start()
            _row_copy(y_hbm, p2_ref[nxt + r], c_bufs[1 - cur], r, sem_c.at[1 - cur]).start()
        z = alpha * h_ref[...] + (w1_ref[...] * a_bufs[cur][...] + w2_ref[...] * c_bufs[cur][...])
        o_ref[...] = _layer_norm_rows(z, g_ref[...], b_ref[...])

        @pl.when(i == last)
        def _():
            wait_rows(1 - cur)

    @pl.when(i % 2 == 0)
    def _():
        step(0)

    @pl.when(i % 2 == 1)
    def _():
        step(1)


def _combine(y, h, pos1, pos2, w1c, w2c, g, b, alpha, tc=256):
    n = h.shape[0]
    kern = functools.partial(_combine_kernel, tc=tc, alpha=alpha)
    row = lambda i, p1, p2: (i, 0)
    fixed = lambda i, p1, p2: (0, 0)
    grid_spec = pltpu.PrefetchScalarGridSpec(
        num_scalar_prefetch=2,
        grid=(n // tc,),
        in_specs=[pl.BlockSpec(memory_space=pl.ANY),
                  pl.BlockSpec((tc, D_MODEL), row),
                  pl.BlockSpec((tc, 1), row), pl.BlockSpec((tc, 1), row),
                  pl.BlockSpec((1, D_MODEL), fixed), pl.BlockSpec((1, D_MODEL), fixed)],
        out_specs=pl.BlockSpec((tc, D_MODEL), row),
        scratch_shapes=[pltpu.VMEM((tc, D_MODEL), F32)] * 4
                       + [pltpu.SemaphoreType.DMA((2,)), pltpu.SemaphoreType.DMA((2,))],
    )
    return pl.pallas_call(
        kern,
        grid_spec=grid_spec,
        out_shape=jax.ShapeDtypeStruct((n, D_MODEL), F32),
        compiler_params=_params("arbitrary"),
        name="moe_combine",
    )(pos1, pos2, y, h, w1c, w2c, g, b)


def _moe(h, rw_t, bias_col, w1, w2, g, b, alpha):
    n = h.shape[0]
    tm = MOE_TILE
    n_tiles = (2 * n) // tm + N_EXPERTS
    ids, wts, cnt = _router(h, rw_t, bias_col)
    counts = cnt[:, 0].astype(I32)
    padded = ((counts + tm - 1) // tm) * tm
    ends = jnp.cumsum(padded)
    offs = ends - padded
    pos1 = offs[ids[0]] + ids[2]
    pos2 = offs[ids[1]] + ids[3]
    tok = jnp.arange(n, dtype=I32)
    src_rows = jnp.zeros((n_tiles * tm,), I32).at[jnp.concatenate([pos1, pos2])].set(
        jnp.concatenate([tok, tok]))
    tile_start = jnp.arange(n_tiles, dtype=I32) * tm
    tile_expert = jnp.minimum(jnp.sum(tile_start[:, None] >= ends[None, :], axis=1),
                              N_EXPERTS - 1).astype(I32)
    n_used = (ends[-1:] // tm).astype(I32)
    y = _experts(h, w1, w2, tile_expert, n_used, src_rows, n_tiles)
    return _combine(y, h, pos1, pos2, wts[0].reshape(n, 1), wts[1].reshape(n, 1), g, b, alpha)


def _rotary_tables(t):
    inv = 1.0 / (ROPE_THETA ** (jnp.arange(0, DA_QK_DIM, 2, dtype=F32) / DA_QK_DIM))
    ang = jnp.arange(t, dtype=F32)[:, None] * inv[None, :]
    cos, sin = jnp.cos(ang), jnp.sin(ang)
    return jnp.tile(cos, (1, 4)), jnp.concatenate([-sin, sin, -sin, sin], axis=1)


def _split_w_in(w):
    main = jnp.concatenate([w[:, :3584], w[:, 3592:7688]], axis=1).astype(BF16)
    zero4 = jnp.zeros((w.shape[0], 4), w.dtype)
    gate = jnp.concatenate([w[:, 3584:3588], zero4, w[:, 3588:3592], zero4, w[:, 7688:7704],
                            jnp.zeros((w.shape[0], HEAD - GATE_ROWS), w.dtype)], axis=1).astype(BF16)
    return main, gate


def kernel(x, mem, w_in, da_lambda, da_norm_g, ml_gate_b, ml_norm_g, gd_conv_w, gd_a_log, gd_dt_bias,
           gd_norm_g, w_out, xa_wq, xa_wkv, xa_wo, router_w, router_bias, moe_w_in, moe_w_out,
           ln_g, ln_b):
    bsz, t, d = x.shape
    mem_len = mem.shape[1]
    depth = w_in.shape[0]
    n = bsz * t
    alpha = (2 * depth) ** 0.25
    cos_t, sin_t = _rotary_tables(t)
    rw_t = router_w.T.astype(BF16)
    bias_col = router_bias.reshape(N_EXPERTS, 1).astype(F32)
    mem2 = mem.reshape(bsz * mem_len, d)
    zero4 = jnp.zeros((4,), F32)
    h = x.reshape(n, d)
    for l in range(depth):
        lambda_init = 0.8 - 0.6 * math.exp(-0.3 * l)
        w_main, w_gate = _split_w_in(w_in[l])
        proj = _matmul(h, w_main, F32, 1024, 512)
        gates = _matmul(h, w_gate, F32, 1024, HEAD)
        gt = gates[:, :GATE_ROWS].reshape(bsz, t, GATE_ROWS).transpose(0, 2, 1)
        consts = jnp.concatenate([ml_gate_b[l, 0], zero4, ml_gate_b[l, 1], zero4,
                                  gd_dt_bias[l], gd_a_log[l]]).reshape(GATE_ROWS, 1)
        r_gate, b_cum, r_max, beta, gc = _gate_prep(gt, consts)
        y_da = _diff_attention(proj, cos_t, sin_t, da_lambda[l], da_norm_g[l].reshape(1, HEAD),
                               lambda_init, bsz, t)
        y_ml = _mlstm(proj, b_cum, r_gate, r_max, ml_norm_g[l].reshape(1, HEAD), bsz, t)
        y_gd = _gated_deltanet(proj, gd_conv_w[l], beta, gc, gd_norm_g[l].reshape(1, HEAD), bsz, t)
        wo = w_out[l].astype(BF16)
        h = _proj_ln([y_da, y_ml, y_gd], [wo[:512], wo[512:1024], wo[1024:]], h,
                     ln_g[l, 0].reshape(1, d), ln_b[l, 0].reshape(1, d), alpha)
        q = _matmul(h, xa_wq[l].astype(BF16), BF16, 1024, 512)
        kv = _matmul(mem2, xa_wkv[l].astype(BF16), BF16, 1024, 512)
        xo = _cross_attention(q, kv, bsz, t, mem_len)
        h = _proj_ln([xo], [xa_wo[l].astype(BF16)], h,
                     ln_g[l, 1].reshape(1, d), ln_b[l, 1].reshape(1, d), alpha)
        h = _moe(h, rw_t, bias_col, moe_w_in[l].astype(BF16), moe_w_out[l].astype(BF16),
                 ln_g[l, 2].reshape(1, d), ln_b[l, 2].reshape(1, d), alpha)
    return h.reshape(bsz, t, d)
```

```python
import functools
import math

import jax
import jax.numpy as jnp
from jax import lax
from jax.experimental import pallas as pl
from jax.experimental.pallas import tpu as pltpu

F32 = jnp.float32
BF16 = jnp.bfloat16
I32 = jnp.int32

D_MODEL = 2048
HEAD = 128
DA_HEADS = 4
DA_QK_DIM = 64
ML_HEADS = 4
GD_HEADS = 8
CHUNK = 64
GD_PACK = 4
CONV_W = 4
ROPE_THETA = 10000.0
XA_HEADS = 4
XA_DIM = D_MODEL // XA_HEADS
N_EXPERTS = 16
N_GROUPS = 4
GROUP_SIZE = 4
D_FF = D_MODEL // 2
MAIN_WIDTH = 7680
GATE_ROWS = 32
NEG = -1e30
VMEM_LIMIT = 56 * 1024 * 1024
MOE_TILE = 512
ATT_BLOCK = 512

CB_DA_Q, CB_DA_K, CB_DA_V = 0, 4, 8
CB_ML_Q, CB_ML_K, CB_ML_V, CB_ML_O = 12, 16, 20, 24
CB_GD_Q, CB_GD_K, CB_GD_V, CB_GD_Z = 28, 36, 44, 52


def _params(*sem):
    return pltpu.CompilerParams(dimension_semantics=sem, vmem_limit_bytes=VMEM_LIMIT)


def _dot(a, b):
    return jnp.dot(a, b, preferred_element_type=F32)


def _dot_nt(a, b):
    return lax.dot_general(a, b, (((1,), (1,)), ((), ())), preferred_element_type=F32)


def _dot_tn(a, b):
    return lax.dot_general(a, b, (((0,), (0,)), ((), ())), preferred_element_type=F32)


def _sigmoid(x):
    return 1.0 / (1.0 + jnp.exp(-x))


def _col_from_row(row, n):
    ri = lax.broadcasted_iota(I32, (n, n), 0)
    ci = lax.broadcasted_iota(I32, (n, n), 1)
    return jnp.sum(jnp.where(ri == ci, jnp.broadcast_to(row, (n, n)), 0.0), axis=1, keepdims=True)


def _lane_cumsum(x, seg):
    lane = lax.broadcasted_iota(I32, x.shape, 1) % seg
    s = 1
    while s < seg:
        x = x + jnp.where(lane >= s, pltpu.roll(x, s, 1), 0.0)
        s *= 2
    return x


def _mm_kernel(x_ref, w_ref, o_ref, xb_ref):
    @pl.when(pl.program_id(1) == 0)
    def _():
        xb_ref[...] = x_ref[...].astype(BF16)

    o_ref[...] = _dot(xb_ref[...], w_ref[...]).astype(o_ref.dtype)


def _matmul(x, w, out_dtype, tm, tn):
    m, k = x.shape
    n = w.shape[1]
    tm = min(tm, m)
    return pl.pallas_call(
        _mm_kernel,
        grid=(m // tm, n // tn),
        in_specs=[pl.BlockSpec((tm, k), lambda i, j: (i, 0)),
                  pl.BlockSpec((k, tn), lambda i, j: (0, j))],
        out_specs=pl.BlockSpec((tm, tn), lambda i, j: (i, j)),
        out_shape=jax.ShapeDtypeStruct((m, n), out_dtype),
        scratch_shapes=[pltpu.VMEM((tm, k), BF16)],
        compiler_params=_params("arbitrary", "arbitrary"),
        name="matmul",
    )(x, w)


def _for_blocks(n, fn):
    def pair(jj, carry):
        fn(2 * jj)
        fn(2 * jj + 1)
        return carry

    lax.fori_loop(0, n // 2, pair, 0)

    @pl.when(n % 2 == 1)
    def _():
        fn(n - 1)


def _lane_cummax(x):
    lane = lax.broadcasted_iota(I32, x.shape, 1)
    s = 1
    while s < x.shape[1]:
        x = jnp.maximum(x, jnp.where(lane >= s, pltpu.roll(x, s, 1), NEG))
        s *= 2
    return x


def _gate_kernel(g_ref, c_ref, r_ref, b_ref, rmax_ref, beta_ref, gc_ref):
    g = g_ref[0]
    c = c_ref[...]
    li = g[0:8] + c[0:8]
    xf = g[8:16] + c[8:16]
    lf = jnp.minimum(xf, 0.0) - jnp.log(1.0 + jnp.exp(-jnp.abs(xf)))
    b = _lane_cumsum(lf, lf.shape[1])
    b_ref[0] = b
    r = li - b
    r_ref[0] = r
    rmax_ref[0] = _lane_cummax(r)
    xa = g[16:24] + c[16:24]
    sp = jnp.maximum(xa, 0.0) + jnp.log(1.0 + jnp.exp(-jnp.abs(xa)))
    gdec = -jnp.exp(c[24:32]) * sp
    gc_ref[0] = _lane_cumsum(gdec, CHUNK)
    beta_ref[0] = _sigmoid(g[24:32])


def _gate_prep(gt, consts):
    bsz, _, t = gt.shape
    out = jax.ShapeDtypeStruct((bsz, 8, t), F32)
    spec = pl.BlockSpec((1, 8, t), lambda b: (b, 0, 0))
    return pl.pallas_call(
        _gate_kernel,
        grid=(bsz,),
        in_specs=[pl.BlockSpec((1, GATE_ROWS, t), lambda b: (b, 0, 0)),
                  pl.BlockSpec((GATE_ROWS, 1), lambda b: (0, 0))],
        out_specs=[spec] * 5,
        out_shape=[out] * 5,
        compiler_params=_params("arbitrary"),
        name="gate_prep",
    )(gt, consts)


def _rotary(t, c, s):
    lane = lax.broadcasted_iota(I32, t.shape, 1)
    swapped = jnp.where((lane % DA_QK_DIM) < DA_QK_DIM // 2,
                        pltpu.roll(t, HEAD - DA_QK_DIM // 2, 1), pltpu.roll(t, DA_QK_DIM // 2, 1))
    return t * c + swapped * s


def _da_kernel(q_ref, k_ref, v_ref, cq_ref, sq_ref, ck_ref, sk_ref, lam_ref, g_ref, o_ref,
               kr_ref, vb_ref, s_ref, mx_ref, ls_ref, acc_ref, *, bq, lambda_init):
    qi = pl.program_id(2)
    nh = bq // HEAD

    @pl.when(qi == 0)
    def _():
        kr_ref[...] = _rotary(k_ref[...], ck_ref[...], sk_ref[...]).astype(BF16)
        vb_ref[...] = v_ref[...].astype(BF16)

    qr = _rotary(q_ref[...], cq_ref[...], sq_ref[...]) * (DA_QK_DIM ** -0.5)
    lane = lax.broadcasted_iota(I32, qr.shape, 1)
    qs = jnp.concatenate([jnp.where(lane < DA_QK_DIM, qr, 0.0),
                          jnp.where(lane >= DA_QK_DIM, qr, 0.0)], axis=0).astype(BF16)

    def fold(x, op):
        out = x[:, :HEAD]
        for i in range(1, nh):
            out = op(out, x[:, i * HEAD:(i + 1) * HEAD])
        return out

    mx_ref[...] = jnp.full(mx_ref.shape, NEG, F32)

    def scores(j):
        off = pl.multiple_of(j * bq, bq)
        s = _dot_nt(qs, kr_ref[pl.ds(off, bq), :])
        s_ref[j] = s
        mx_ref[...] = jnp.maximum(mx_ref[...], fold(s, jnp.maximum))

    _for_blocks(qi, scores)
    offd = pl.multiple_of(qi * bq, bq)
    sd = _dot_nt(qs, kr_ref[pl.ds(offd, bq), :])
    r = lax.broadcasted_iota(I32, sd.shape, 0)
    c = lax.broadcasted_iota(I32, sd.shape, 1)
    sd = jnp.where(jnp.where(r >= bq, r - bq, r) >= c, sd, NEG)
    m = jnp.max(jnp.maximum(mx_ref[...], fold(sd, jnp.maximum)), axis=1, keepdims=True)
    mb = jnp.broadcast_to(m, (2 * bq, HEAD))

    ls_ref[...] = jnp.zeros(ls_ref.shape, F32)
    acc_ref[...] = jnp.zeros(acc_ref.shape, F32)

    def accumulate(s, off):
        ps = [jnp.exp(s[:, i * HEAD:(i + 1) * HEAD] - mb) for i in range(nh)]
        tot = ps[0]
        for p in ps[1:]:
            tot = tot + p
        ls_ref[...] += tot
        pb = jnp.concatenate([p.astype(BF16) for p in ps], axis=1)
        acc_ref[...] += _dot(pb, vb_ref[pl.ds(off, bq), :])

    _for_blocks(qi, lambda j: accumulate(s_ref[j], pl.multiple_of(j * bq, bq)))
    accumulate(sd, offd)

    lam = lam_ref[...]
    s1 = jnp.sum(lam[0:1] * lam[1:2], axis=1, keepdims=True)
    s2 = jnp.sum(lam[2:3] * lam[3:4], axis=1, keepdims=True)
    lam_full = jnp.exp(s1) - jnp.exp(s2) + lambda_init
    acc = acc_ref[...]
    l = jnp.sum(ls_ref[...], axis=1, keepdims=True)
    o = acc[:bq] / l[:bq] - lam_full * (acc[bq:] / l[bq:])
    o = o * lax.rsqrt(jnp.mean(o * o, axis=1, keepdims=True) + 1e-6) * g_ref[...]
    o_ref[...] = (o * (1.0 - lambda_init)).astype(o_ref.dtype)


def _diff_attention(proj, cos_t, sin_t, lam, norm_g, lambda_init, bsz, t):
    bq = ATT_BLOCK
    nq = t // bq
    kern = functools.partial(_da_kernel, bq=bq, lambda_init=lambda_init)
    tab_q = pl.BlockSpec((bq, HEAD), lambda b, h, i: (i, 0))
    tab_k = pl.BlockSpec((t, HEAD), lambda b, h, i: (0, 0))
    return pl.pallas_call(
        kern,
        grid=(bsz, DA_HEADS, nq),
        in_specs=[pl.BlockSpec((bq, HEAD), lambda b, h, i: (b * nq + i, CB_DA_Q + h)),
                  pl.BlockSpec((t, HEAD), lambda b, h, i: (b, CB_DA_K + h)),
                  pl.BlockSpec((t, HEAD), lambda b, h, i: (b, CB_DA_V + h)),
                  tab_q, tab_q, tab_k, tab_k,
                  pl.BlockSpec((4, DA_QK_DIM), lambda b, h, i: (0, 0)),
                  pl.BlockSpec((1, HEAD), lambda b, h, i: (0, 0))],
        out_specs=pl.BlockSpec((bq, HEAD), lambda b, h, i: (b * nq + i, h)),
        out_shape=jax.ShapeDtypeStruct((bsz * t, DA_HEADS * HEAD), BF16),
        scratch_shapes=[pltpu.VMEM((t, HEAD), BF16), pltpu.VMEM((t, HEAD), BF16),
                        pltpu.VMEM((nq, 2 * bq, bq), F32),
                        pltpu.VMEM((2 * bq, HEAD), F32), pltpu.VMEM((2 * bq, HEAD), F32),
                        pltpu.VMEM((2 * bq, HEAD), F32)],
        compiler_params=_params("arbitrary", "arbitrary", "arbitrary"),
        name="diff_attention",
    )(proj, proj, proj, cos_t, sin_t, cos_t, sin_t, lam, norm_g)


def _ml_kernel(q_ref, k_ref, v_ref, op_ref, b_ref, r_ref, rmax_ref, g_ref, o_ref,
               kb_ref, vb_ref, den_ref, num_ref, *, bq):
    qi = pl.program_id(2)
    nh = bq // HEAD

    @pl.when(qi == 0)
    def _():
        kb_ref[...] = (k_ref[...] * (HEAD ** -0.5)).astype(BF16)
        vb_ref[...] = v_ref[...].astype(BF16)

    qb = q_ref[...].astype(BF16)
    b_col = _col_from_row(b_ref[0, 0, pl.ds(qi, 1), :], bq)
    rmax_col = _col_from_row(rmax_ref[0, 0, pl.ds(qi, 1), :], bq)
    den_ref[...] = jnp.zeros(den_ref.shape, F32)
    num_ref[...] = jnp.zeros(num_ref.shape, F32)

    def block(j, masked):
        off = pl.multiple_of(j * bq, bq)
        s = _dot_nt(qb, kb_ref[pl.ds(off, bq), :])
        logw = r_ref[0, 0, pl.ds(j, 1), :] - rmax_col
        if masked:
            r = lax.broadcasted_iota(I32, s.shape, 0)
            c = lax.broadcasted_iota(I32, s.shape, 1)
            logw = jnp.where(r >= c, logw, NEG)
        sw = s * jnp.exp(logw)
        tot = sw[:, :HEAD]
        for i in range(1, nh):
            tot = tot + sw[:, i * HEAD:(i + 1) * HEAD]
        den_ref[...] += tot
        num_ref[...] += _dot(sw.astype(BF16), vb_ref[pl.ds(off, bq), :])

    _for_blocks(qi, lambda j: block(j, False))
    block(qi, True)

    den = jnp.sum(den_ref[...], axis=1, keepdims=True)
    h = num_ref[...] / jnp.maximum(jnp.abs(den), jnp.exp(-(b_col + rmax_col)))
    h = h * lax.rsqrt(jnp.mean(h * h, axis=1, keepdims=True) + 1e-6) * g_ref[...]
    o_ref[...] = (_sigmoid(op_ref[...]) * h).astype(o_ref.dtype)


def _mlstm(proj, b_cum, r_gate, r_max, norm_g, bsz, t):
    bq = ATT_BLOCK
    nq = t // bq
    kern = functools.partial(_ml_kernel, bq=bq)
    gate = pl.BlockSpec((1, 1, nq, bq), lambda b, h, i: (b, h, 0, 0))
    return pl.pallas_call(
        kern,
        grid=(bsz, ML_HEADS, nq),
        in_specs=[pl.BlockSpec((bq, HEAD), lambda b, h, i: (b * nq + i, CB_ML_Q + h)),
                  pl.BlockSpec((t, HEAD), lambda b, h, i: (b, CB_ML_K + h)),
                  pl.BlockSpec((t, HEAD), lambda b, h, i: (b, CB_ML_V + h)),
                  pl.BlockSpec((bq, HEAD), lambda b, h, i: (b * nq + i, CB_ML_O + h)),
                  gate, gate, gate,
                  pl.BlockSpec((1, HEAD), lambda b, h, i: (0, 0))],
        out_specs=pl.BlockSpec((bq, HEAD), lambda b, h, i: (b * nq + i, h)),
        out_shape=jax.ShapeDtypeStruct((bsz * t, ML_HEADS * HEAD), BF16),
        scratch_shapes=[pltpu.VMEM((t, HEAD), BF16), pltpu.VMEM((t, HEAD), BF16),
                        pltpu.VMEM((bq, HEAD), F32), pltpu.VMEM((bq, HEAD), F32)],
        compiler_params=_params("arbitrary", "arbitrary", "arbitrary"),
        name="mlstm",
    )(proj, proj, proj, proj, b_cum.reshape(bsz, 8, nq, bq), r_gate.reshape(bsz, 8, nq, bq),
      r_max.reshape(bsz, 8, nq, bq), norm_g)


def _gd_kernel(q_ref, k_ref, v_ref, z_ref, cwq_ref, cwk_ref, cwv_ref, beta_ref, gc_ref, g_ref, o_ref,
               mq_ref, no_ref, egl_ref, *, t, heads):
    L = CHUNK
    S2 = GD_PACK * L
    nc = t // L
    ri = lax.broadcasted_iota(I32, (S2, S2), 0)
    ci = lax.broadcasted_iota(I32, (S2, S2), 1)
    same = (ri // L) == (ci // L)
    tri = same & (ri >= ci)
    strict = same & (ri > ci)
    eye = (ri == ci).astype(F32)
    row_chunk = lax.broadcasted_iota(I32, (S2, 1), 0) // L

    def conv_silu(x_ref, cw_ref, lo, sc, base):
        cur = x_ref[pl.ds(base, S2), lo:lo + HEAD]
        prev = x_ref[pl.ds(pl.multiple_of(jnp.maximum(base - 8, 0), 8), 8), lo:lo + HEAD]
        blk = jnp.concatenate([jnp.where(sc > 0, prev, 0.0), cur], axis=0)
        cw = cw_ref[:, lo:lo + HEAD]
        y = blk[5:5 + S2] * cw[0:1]
        for i in range(1, CONV_W):
            y = y + blk[5 + i:5 + i + S2] * cw[i:i + 1]
        return y * _sigmoid(y)

    def mm(a, b):
        return _dot(a.astype(BF16), b.astype(BF16))

    def local_head(hh, sc):
        lo = hh * HEAD
        base = pl.multiple_of(sc * S2, S2)
        q = conv_silu(q_ref, cwq_ref, lo, sc, base)
        k = conv_silu(k_ref, cwk_ref, lo, sc, base)
        v = conv_silu(v_ref, cwv_ref, lo, sc, base)
        q = q * lax.rsqrt(jnp.sum(q * q, axis=1, keepdims=True) + 1e-6) * (HEAD ** -0.5)
        k = k * lax.rsqrt(jnp.sum(k * k, axis=1, keepdims=True) + 1e-6)
        gc_row = gc_ref[0, hh, pl.ds(sc, 1), :]
        gc_col = _col_from_row(gc_row, S2)
        beta_col = _col_from_row(beta_ref[0, hh, pl.ds(sc, 1), :], S2)
        decay = jnp.exp(jnp.where(tri, gc_col - gc_row, NEG))
        kb = k * beta_col
        kq = _dot_nt(jnp.concatenate([kb, q], axis=0).astype(BF16), k.astype(BF16))
        a = jnp.where(strict, kq[:S2] * decay, 0.0)
        attn = jnp.where(tri, kq[S2:] * decay, 0.0).astype(BF16)
        x = eye - a
        p = mm(a, a)
        for _ in range(4):
            px = mm(jnp.concatenate([p, x], axis=0), p)
            p = px[:S2]
            x = x + px[S2:]
        x = x + mm(x, p)
        rhs = jnp.concatenate([kb * jnp.exp(gc_col), v * beta_col], axis=1)
        wu = mm(x, rhs)
        gls = [gc_row[:, c * L + L - 1:c * L + L] for c in range(GD_PACK)]
        gl_col = gls[-1]
        for c in reversed(range(GD_PACK - 1)):
            gl_col = jnp.where(row_chunk == c, gls[c], gl_col)
        kd = (k * jnp.exp(gl_col - gc_col)).astype(BF16)
        aw = _dot(attn, wu.astype(BF16))
        qprime = (q * jnp.exp(gc_col) - aw[:, :HEAD]).astype(BF16)
        wbd = jnp.concatenate([jnp.where(row_chunk == c, wu, 0.0).astype(BF16) for c in range(GD_PACK)],
                              axis=1)
        mn = _dot_tn(kd, wbd)
        for c in range(GD_PACK):
            rows = slice(c * L, (c + 1) * L)
            blk = pl.multiple_of((GD_PACK * sc + c) * (HEAD + L), L)
            mq_ref[hh, pl.ds(blk, HEAD), :] = mn[:, 2 * c * HEAD:(2 * c + 1) * HEAD].astype(BF16)
            mq_ref[hh, pl.ds(blk + HEAD, L), :] = qprime[rows]
            no_ref[hh, pl.ds(blk, HEAD), :] = mn[:, (2 * c + 1) * HEAD:(2 * c + 2) * HEAD]
            no_ref[hh, pl.ds(blk + HEAD, L), :] = aw[rows, HEAD:]
            egl_ref[hh, pl.ds(GD_PACK * sc + c, 1), :] = jnp.broadcast_to(jnp.exp(gls[c]), (1, HEAD))

    def local(sc, carry):
        for hh in range(heads):
            local_head(hh, sc)
        return carry

    lax.fori_loop(0, nc // GD_PACK, local, 0, unroll=2)

    gn = g_ref[...]

    def step(c, states):
        base = pl.multiple_of(c * L, L)
        blk = pl.multiple_of(c * (HEAD + L), L)
        new = []
        for hh in range(heads):
            lo = hh * HEAD
            s = states[hh]
            ms = _dot(mq_ref[hh, pl.ds(blk, HEAD + L), :], s.astype(BF16))
            new.append(s * egl_ref[hh, pl.ds(c, 1), :] - ms[:HEAD] + no_ref[hh, pl.ds(blk, HEAD), :])
            o = ms[HEAD:] + no_ref[hh, pl.ds(blk + HEAD, L), :]
            o = o * lax.rsqrt(jnp.mean(o * o, axis=1, keepdims=True) + 1e-6) * gn
            z = z_ref[pl.ds(base, L), lo:lo + HEAD]
            o_ref[pl.ds(base, L), lo:lo + HEAD] = (o * (z * _sigmoid(z))).astype(o_ref.dtype)
        return tuple(new)

    lax.fori_loop(0, nc, step, tuple(jnp.zeros((HEAD, HEAD), F32) for _ in range(heads)), unroll=2)


def _gated_deltanet(proj, conv_w, beta, gc, norm_g, bsz, t, heads=2):
    nc = t // CHUNK
    width = heads * HEAD
    kern = functools.partial(_gd_kernel, t=t, heads=heads)

    def slab(cb):
        return pl.BlockSpec((t, width), lambda b, h: (b, cb // heads + h))

    def cw(off):
        return pl.BlockSpec((CONV_W, width), lambda b, h: (0, off // heads + h))

    nsc, srows = nc // GD_PACK, GD_PACK * CHUNK
    gate = pl.BlockSpec((1, heads, nsc, srows), lambda b, h: (b, h, 0, 0))
    return pl.pallas_call(
        kern,
        grid=(bsz, GD_HEADS // heads),
        in_specs=[slab(CB_GD_Q), slab(CB_GD_K), slab(CB_GD_V), slab(CB_GD_Z),
                  cw(0), cw(GD_HEADS), cw(2 * GD_HEADS), gate, gate,
                  pl.BlockSpec((1, HEAD), lambda b, h: (0, 0))],
        out_specs=pl.BlockSpec((t, width), lambda b, h: (b, h)),
        out_shape=jax.ShapeDtypeStruct((bsz * t, GD_HEADS * HEAD), BF16),
        scratch_shapes=[pltpu.VMEM((heads, nc * (HEAD + CHUNK), HEAD), BF16),
                        pltpu.VMEM((heads, nc * (HEAD + CHUNK), HEAD), F32),
                        pltpu.VMEM((heads, nc, HEAD), F32)],
        compiler_params=_params("arbitrary", "arbitrary"),
        name="gated_deltanet",
    )(proj, proj, proj, proj, conv_w, conv_w, conv_w,
      beta.reshape(bsz, 8, nsc, srows), gc.reshape(bsz, 8, nsc, srows), norm_g)


def _layer_norm_rows(z, g, b):
    mu = jnp.mean(z, axis=1, keepdims=True)
    zc = z - mu
    var = jnp.mean(zc * zc, axis=1, keepdims=True)
    return zc * lax.rsqrt(var + 1e-5) * g + b


def _proj_ln_kernel(*refs, nx, alpha, tn):
    x_refs = refs[:nx]
    w_refs = refs[nx:2 * nx]
    h_ref, g_ref, b_ref, o_ref, z_ref = refs[2 * nx:]
    for n in range(0, D_MODEL, tn):
        acc = alpha * h_ref[:, n:n + tn]
        for x_ref, w_ref in zip(x_refs, w_refs):
            acc = acc + _dot(x_ref[...], w_ref[:, n:n + tn])
        z_ref[:, n:n + tn] = acc
    o_ref[...] = _layer_norm_rows(z_ref[...], g_ref[...], b_ref[...])


def _proj_ln(xs, w, h, g, b, alpha, tm=512):
    n_rows = h.shape[0]
    nx = len(xs)
    kern = functools.partial(_proj_ln_kernel, nx=nx, alpha=alpha, tn=512)
    row = lambda i: (i, 0)
    fixed = lambda i: (0, 0)
    w_specs, start = [], 0
    for x in xs:
        width = x.shape[1]
        w_specs.append(pl.BlockSpec((width, D_MODEL), functools.partial(lambda i, blk: (blk, 0),
                                                                         blk=start // width)))
        start += width
    in_specs = ([pl.BlockSpec((tm, x.shape[1]), row) for x in xs]
                + w_specs
                + [pl.BlockSpec((tm, D_MODEL), row),
                   pl.BlockSpec((1, D_MODEL), fixed), pl.BlockSpec((1, D_MODEL), fixed)])
    return pl.pallas_call(
        kern,
        grid=(n_rows // tm,),
        in_specs=in_specs,
        out_specs=pl.BlockSpec((tm, D_MODEL), row),
        out_shape=jax.ShapeDtypeStruct((n_rows, D_MODEL), F32),
        scratch_shapes=[pltpu.VMEM((tm, D_MODEL), F32)],
        compiler_params=_params("arbitrary"),
        name="proj_ln",
    )(*xs, *([w] * nx), h, g, b)


def _xa_kernel(q_ref, kv_ref, o_ref):
    for hh in range(XA_HEADS):
        lo = hh * XA_DIM
        s = _dot_nt(q_ref[:, lo:lo + XA_DIM], kv_ref[:, lo:lo + XA_DIM]) * (XA_DIM ** -0.5)
        s = s - jnp.max(s, axis=1, keepdims=True)
        p = jnp.exp(s)
        p = p / jnp.sum(p, axis=1, keepdims=True)
        o = _dot(p.astype(BF16), kv_ref[:, D_MODEL + lo:D_MODEL + lo + XA_DIM])
        o_ref[:, lo:lo + XA_DIM] = o.astype(o_ref.dtype)


def _cross_attention(q, kv, bsz, t, mem_len, tq=512):
    nq = t // tq
    return pl.pallas_call(
        _xa_kernel,
        grid=(bsz, nq),
        in_specs=[pl.BlockSpec((tq, D_MODEL), lambda b, i: (b * nq + i, 0)),
                  pl.BlockSpec((mem_len, 2 * D_MODEL), lambda b, i: (b, 0))],
        out_specs=pl.BlockSpec((tq, D_MODEL), lambda b, i: (b * nq + i, 0)),
        out_shape=jax.ShapeDtypeStruct((bsz * t, D_MODEL), BF16),
        compiler_params=_params("arbitrary", "arbitrary"),
        name="cross_attention",
    )(q, kv)


def _router_kernel(x_ref, rw_ref, bias_ref, ids_ref, wts_ref, cnt_ref, carry_ref, *, tm):
    i = pl.program_id(0)

    @pl.when(i == 0)
    def _():
        carry_ref[...] = jnp.zeros(carry_ref.shape, F32)

    logits = _dot_nt(rw_ref[...], x_ref[...].astype(BF16))
    scores = _sigmoid(logits)
    sel = scores + bias_ref[...]
    srow = [sel[e:e + 1] for e in range(N_EXPERTS)]
    prow = [scores[e:e + 1] for e in range(N_EXPERTS)]

    gscore = []
    for g in range(N_GROUPS):
        s0, s1, s2, s3 = srow[4 * g:4 * g + 4]
        a, b = jnp.maximum(s0, s1), jnp.minimum(s0, s1)
        c, d = jnp.maximum(s2, s3), jnp.minimum(s2, s3)
        gscore.append(jnp.maximum(a, c) + jnp.maximum(jnp.minimum(a, c), jnp.maximum(b, d)))
    gmax = jnp.maximum(jnp.maximum(gscore[0], gscore[1]), jnp.maximum(gscore[2], gscore[3]))
    gi = jnp.where(gscore[0] == gmax, 0, jnp.where(gscore[1] == gmax, 1,
                                                    jnp.where(gscore[2] == gmax, 2, 3)))

    def pick(rows, j):
        return jnp.where(gi == 0, rows[j], jnp.where(gi == 1, rows[4 + j],
                                                      jnp.where(gi == 2, rows[8 + j], rows[12 + j])))

    v = [pick(srow, j) for j in range(GROUP_SIZE)]
    sc = [pick(prow, j) for j in range(GROUP_SIZE)]
    v1 = jnp.maximum(jnp.maximum(v[0], v[1]), jnp.maximum(v[2], v[3]))
    i1 = jnp.where(v[0] == v1, 0, jnp.where(v[1] == v1, 1, jnp.where(v[2] == v1, 2, 3)))
    rest = [jnp.where(i1 == j, -jnp.inf, v[j]) for j in range(GROUP_SIZE)]
    v2 = jnp.maximum(jnp.maximum(rest[0], rest[1]), jnp.maximum(rest[2], rest[3]))
    i2 = jnp.where(rest[0] == v2, 0, jnp.where(rest[1] == v2, 1, jnp.where(rest[2] == v2, 2, 3)))

    def pick_idx(idx):
        return jnp.where(idx == 0, sc[0], jnp.where(idx == 1, sc[1], jnp.where(idx == 2, sc[2], sc[3])))

    p1, p2 = pick_idx(i1), pick_idx(i2)
    tot = p1 + p2
    e1 = 4 * gi + i1
    e2 = 4 * gi + i2

    erow = lax.broadcasted_iota(I32, (N_EXPERTS, tm), 0)
    hit1 = erow == e1
    hit2 = erow == e2
    onehot = jnp.where(hit1 | hit2, 1.0, 0.0)
    incl = _lane_cumsum(onehot, tm)
    excl = incl - onehot + carry_ref[...]
    rank1 = jnp.sum(jnp.where(hit1, excl, 0.0), axis=0, keepdims=True)
    rank2 = jnp.sum(jnp.where(hit2, excl, 0.0), axis=0, keepdims=True)
    carry_ref[...] = carry_ref[...] + incl[:, tm - 1:tm]

    zi = jnp.zeros((4, tm), I32)
    ids_ref[...] = jnp.concatenate([e1, e2, rank1.astype(I32), rank2.astype(I32), zi], axis=0)
    wts_ref[...] = jnp.concatenate([p1 / tot, p2 / tot, jnp.zeros((6, tm), F32)], axis=0)
    cnt_ref[...] = jnp.broadcast_to(carry_ref[...], cnt_ref.shape)


def _router(h, rw_t, bias_col, tm=2048):
    n = h.shape[0]
    tm = min(tm, n)
    kern = functools.partial(_router_kernel, tm=tm)
    return pl.pallas_call(
        kern,
        grid=(n // tm,),
        in_specs=[pl.BlockSpec((tm, D_MODEL), lambda i: (i, 0)),
                  pl.BlockSpec((N_EXPERTS, D_MODEL), lambda i: (0, 0)),
                  pl.BlockSpec((N_EXPERTS, 1), lambda i: (0, 0))],
        out_specs=[pl.BlockSpec((8, tm), lambda i: (0, i)),
                   pl.BlockSpec((8, tm), lambda i: (0, i)),
                   pl.BlockSpec((N_EXPERTS, HEAD), lambda i: (0, 0))],
        out_shape=[jax.ShapeDtypeStruct((8, n), I32), jax.ShapeDtypeStruct((8, n), F32),
                   jax.ShapeDtypeStruct((N_EXPERTS, HEAD), F32)],
        scratch_shapes=[pltpu.VMEM((N_EXPERTS, 1), F32)],
        compiler_params=_params("arbitrary"),
        name="router",
    )(h, rw_t, bias_col)


def _row_copy(src_hbm, row, dst_ref, slot, sem):
    return pltpu.make_async_copy(src_hbm.at[pl.ds(row, 1), :], dst_ref.at[pl.ds(slot, 1), :], sem)


def _expert_kernel(texp_ref, nused_ref, src_ref, h_hbm, w1_ref, w2_ref, o_ref, xa_ref, xb_ref, sem, *, tm):
    i = pl.program_id(0)
    n_used = nused_ref[0]
    bufs = (xa_ref, xb_ref)

    def wait_rows(buf):
        pltpu.make_async_copy(h_hbm.at[pl.ds(0, tm), :], bufs[buf], sem.at[buf]).wait()

    @pl.when(i == 0)
    def _():
        def body(r, carry):
            _row_copy(h_hbm, src_ref[r], xa_ref, r, sem.at[0]).start()
            return carry

        lax.fori_loop(0, tm, body, 0, unroll=8)

    def tile(cur):
        wait_rows(cur)
        nxt = jnp.minimum(i + 1, n_used - 1) * tm
        for r in range(tm):
            _row_copy(h_hbm, src_ref[nxt + r], bufs[1 - cur], r, sem.at[1 - cur]).start()
        mid = _dot(bufs[cur][...].astype(BF16), w1_ref[0])
        gate = mid[:, :D_FF]
        act = (gate * _sigmoid(gate)) * mid[:, D_FF:]
        o_ref[...] = _dot(act.astype(BF16), w2_ref[0])

        @pl.when(i + 1 == n_used)
        def _():
            wait_rows(1 - cur)

    @pl.when((i < n_used) & (i % 2 == 0))
    def _():
        tile(0)

    @pl.when((i < n_used) & (i % 2 == 1))
    def _():
        tile(1)

    @pl.when(i >= n_used)
    def _():
        o_ref[...] = jnp.zeros(o_ref.shape, F32)


def _experts(h, w1, w2, tile_expert, n_used, src_rows, n_tiles):
    tm = MOE_TILE
    kern = functools.partial(_expert_kernel, tm=tm)
    grid_spec = pltpu.PrefetchScalarGridSpec(
        num_scalar_prefetch=3,
        grid=(n_tiles,),
        in_specs=[pl.BlockSpec(memory_space=pl.ANY),
                  pl.BlockSpec((1, D_MODEL, 2 * D_FF), lambda i, te, nu, sr: (te[i], 0, 0)),
                  pl.BlockSpec((1, D_FF, D_MODEL), lambda i, te, nu, sr: (te[i], 0, 0))],
        out_specs=pl.BlockSpec((tm, D_MODEL), lambda i, te, nu, sr: (i, 0)),
        scratch_shapes=[pltpu.VMEM((tm, D_MODEL), F32), pltpu.VMEM((tm, D_MODEL), F32),
                        pltpu.SemaphoreType.DMA((2,))],
    )
    return pl.pallas_call(
        kern,
        grid_spec=grid_spec,
        out_shape=jax.ShapeDtypeStruct((n_tiles * tm, D_MODEL), F32),
        compiler_params=_params("arbitrary"),
        name="experts",
    )(tile_expert, n_used, src_rows, h, w1, w2)


def _combine_kernel(p1_ref, p2_ref, y_hbm, h_ref, w1_ref, w2_ref, g_ref, b_ref, o_ref,
                    a0_ref, a1_ref, c0_ref, c1_ref, sem_a, sem_c, *, tc, alpha):
    i = pl.program_id(0)
    last = pl.num_programs(0) - 1
    a_bufs = (a0_ref, a1_ref)
    c_bufs = (c0_ref, c1_ref)

    def wait_rows(buf):
        pltpu.make_async_copy(y_hbm.at[pl.ds(0, tc), :], a_bufs[buf], sem_a.at[buf]).wait()
        pltpu.make_async_copy(y_hbm.at[pl.ds(0, tc), :], c_bufs[buf], sem_c.at[buf]).wait()

    @pl.when(i == 0)
    def _():
        def body(r, carry):
            _row_copy(y_hbm, p1_ref[r], a0_ref, r, sem_a.at[0]).start()
            _row_copy(y_hbm, p2_ref[r], c0_ref, r, sem_c.at[0]).start()
            return carry

        lax.fori_loop(0, tc, body, 0, unroll=8)

    def step(cur):
        wait_rows(cur)
        nxt = jnp.minimum(i + 1, last) * tc
        for r in range(tc):
            _row_copy(y_hbm, p1_ref[nxt + r], a_bufs[1 - cur], r, sem_a.at[1 - cur]).start()
            _row_copy(y_hbm, p2_ref[nxt + r], c_bufs[1 - cur], r, sem_c.at[1 - cur]).start()
        z = alpha * h_ref[...] + (w1_ref[...] * a_bufs[cur][...] + w2_ref[...] * c_bufs[cur][...])
        o_ref[...] = _layer_norm_rows(z, g_ref[...], b_ref[...])

        @pl.when(i == last)
        def _():
            wait_rows(1 - cur)

    @pl.when(i % 2 == 0)
    def _():
        step(0)

    @pl.when(i % 2 == 1)
    def _():
        step(1)


def _combine(y, h, pos1, pos2, w1c, w2c, g, b, alpha, tc=256):
    n = h.shape[0]
    kern = functools.partial(_combine_kernel, tc=tc, alpha=alpha)
    row = lambda i, p1, p2: (i, 0)
    fixed = lambda i, p1, p2: (0, 0)
    grid_spec = pltpu.PrefetchScalarGridSpec(
        num_scalar_prefetch=2,
        grid=(n // tc,),
        in_specs=[pl.BlockSpec(memory_space=pl.ANY),
                  pl.BlockSpec((tc, D_MODEL), row),
                  pl.BlockSpec((tc, 1), row), pl.BlockSpec((tc, 1), row),
                  pl.BlockSpec((1, D_MODEL), fixed), pl.BlockSpec((1, D_MODEL), fixed)],
        out_specs=pl.BlockSpec((tc, D_MODEL), row),
        scratch_shapes=[pltpu.VMEM((tc, D_MODEL), F32)] * 4
                       + [pltpu.SemaphoreType.DMA((2,)), pltpu.SemaphoreType.DMA((2,))],
    )
    return pl.pallas_call(
        kern,
        grid_spec=grid_spec,
        out_shape=jax.ShapeDtypeStruct((n, D_MODEL), F32),
        compiler_params=_params("arbitrary"),
        name="moe_combine",
    )(pos1, pos2, y, h, w1c, w2c, g, b)


def _src_rows_kernel(p1_ref, p2_ref, lo_ref, hi_ref, o_ref):
    def clear_range(e, carry):
        def clear(r, c):
            o_ref[r] = 0
            return c

        lax.fori_loop(lo_ref[e], hi_ref[e], clear, 0)
        return carry

    lax.fori_loop(0, lo_ref.shape[0], clear_range, 0)

    def place(tok, carry):
        o_ref[p1_ref[tok]] = tok
        o_ref[p2_ref[tok]] = tok
        return carry

    lax.fori_loop(0, p1_ref.shape[0], place, 0, unroll=8)


def _src_rows(pos1, pos2, pad_lo, pad_hi, total):
    smem = pl.BlockSpec(memory_space=pltpu.SMEM)
    return pl.pallas_call(
        _src_rows_kernel,
        in_specs=[smem, smem, smem, smem],
        out_specs=smem,
        out_shape=jax.ShapeDtypeStruct((total,), I32),
        name="expert_row_sources",
    )(pos1, pos2, pad_lo, pad_hi)


def _moe(h, rw_t, bias_col, w1, w2, g, b, alpha):
    n = h.shape[0]
    tm = MOE_TILE
    n_tiles = (2 * n) // tm + N_EXPERTS
    ids, wts, cnt = _router(h, rw_t, bias_col)
    counts = cnt[:, 0].astype(I32)
    padded = ((counts + tm - 1) // tm) * tm
    ends = jnp.cumsum(padded)
    offs = ends - padded
    pos1 = offs[ids[0]] + ids[2]
    pos2 = offs[ids[1]] + ids[3]
    total = n_tiles * tm
    pad_lo = jnp.concatenate([offs + counts, ends[-1:]]).astype(I32)
    pad_hi = jnp.concatenate([ends, jnp.full((1,), total, I32)]).astype(I32)
    src_rows = _src_rows(pos1, pos2, pad_lo, pad_hi, total)
    tile_start = jnp.arange(n_tiles, dtype=I32) * tm
    tile_expert = jnp.minimum(jnp.sum(tile_start[:, None] >= ends[None, :], axis=1),
                              N_EXPERTS - 1).astype(I32)
    n_used = (ends[-1:] // tm).astype(I32)
    y = _experts(h, w1, w2, tile_expert, n_used, src_rows, n_tiles)
    return _combine(y, h, pos1, pos2, wts[0].reshape(n, 1), wts[1].reshape(n, 1), g, b, alpha)


def _rotary_tables(t):
    inv = 1.0 / (ROPE_THETA ** (jnp.arange(0, DA_QK_DIM, 2, dtype=F32) / DA_QK_DIM))
    ang = jnp.arange(t, dtype=F32)[:, None] * inv[None, :]
    cos, sin = jnp.cos(ang), jnp.sin(ang)
    return jnp.tile(cos, (1, 4)), jnp.concatenate([-sin, sin, -sin, sin], axis=1)


def _split_w_in(w):
    main = jnp.concatenate([w[:, :3584], w[:, 3592:7688]], axis=1).astype(BF16)
    zero4 = jnp.zeros((w.shape[0], 4), w.dtype)
    gate = jnp.concatenate([w[:, 3584:3588], zero4, w[:, 3588:3592], zero4, w[:, 7688:7704],
                            jnp.zeros((w.shape[0], HEAD - GATE_ROWS), w.dtype)], axis=1).astype(BF16)
    return main, gate


def kernel(x, mem, w_in, da_lambda, da_norm_g, ml_gate_b, ml_norm_g, gd_conv_w, gd_a_log, gd_dt_bias,
           gd_norm_g, w_out, xa_wq, xa_wkv, xa_wo, router_w, router_bias, moe_w_in, moe_w_out,
           ln_g, ln_b):
    bsz, t, d = x.shape
    mem_len = mem.shape[1]
    depth = w_in.shape[0]
    n = bsz * t
    alpha = (2 * depth) ** 0.25
    cos_t, sin_t = _rotary_tables(t)
    rw_t = router_w.T.astype(BF16)
    bias_col = router_bias.reshape(N_EXPERTS, 1).astype(F32)
    mem2 = mem.reshape(bsz * mem_len, d)
    zero4 = jnp.zeros((4,), F32)
    h = x.reshape(n, d)
    for l in range(depth):
        lambda_init = 0.8 - 0.6 * math.exp(-0.3 * l)
        w_main, w_gate = _split_w_in(w_in[l])
        proj = _matmul(h, w_main, F32, 1024, 512)
        gates = _matmul(h, w_gate, F32, 1024, HEAD)
        gt = gates[:, :GATE_ROWS].reshape(bsz, t, GATE_ROWS).transpose(0, 2, 1)
        consts = jnp.concatenate([ml_gate_b[l, 0], zero4, ml_gate_b[l, 1], zero4,
                                  gd_dt_bias[l], gd_a_log[l]]).reshape(GATE_ROWS, 1)
        r_gate, b_cum, r_max, beta, gc = _gate_prep(gt, consts)
        y_da = _diff_attention(proj, cos_t, sin_t, da_lambda[l], da_norm_g[l].reshape(1, HEAD),
                               lambda_init, bsz, t)
        y_ml = _mlstm(proj, b_cum, r_gate, r_max, ml_norm_g[l].reshape(1, HEAD), bsz, t)
        y_gd = _gated_deltanet(proj, gd_conv_w[l], beta, gc, gd_norm_g[l].reshape(1, HEAD), bsz, t)
        h = _proj_ln([y_da, y_ml, y_gd], w_out[l].astype(BF16), h,
                     ln_g[l, 0].reshape(1, d), ln_b[l, 0].reshape(1, d), alpha)
        q = _matmul(h, xa_wq[l].astype(BF16), BF16, 1024, 512)
        kv = _matmul(mem2, xa_wkv[l].astype(BF16), BF16, 1024, 512)
        xo = _cross_attention(q, kv, bsz, t, mem_len)
        h = _proj_ln([xo], xa_wo[l].astype(BF16), h,
                     ln_g[l, 1].reshape(1, d), ln_b[l, 1].reshape(1, d), alpha)
        h = _moe(h, rw_t, bias_col, moe_w_in[l].astype(BF16), moe_w_out[l].astype(BF16),
                 ln_g[l, 2].reshape(1, d), ln_b[l, 2].reshape(1, d), alpha)
    return h.reshape(bsz, t, d)
```

```python
import functools
import math

import jax
import jax.numpy as jnp
from jax import lax
from jax.experimental import pallas as pl
from jax.experimental.pallas import tpu as pltpu

F32 = jnp.float32
BF16 = jnp.bfloat16
I32 = jnp.int32

D_MODEL = 2048
HEAD = 128
DA_HEADS = 4
DA_QK_DIM = 64
ML_HEADS = 4
GD_HEADS = 8
CHUNK = 64
GD_PACK = 4
CONV_W = 4
ROPE_THETA = 10000.0
XA_HEADS = 4
XA_DIM = D_MODEL // XA_HEADS
N_EXPERTS = 16
N_GROUPS = 4
GROUP_SIZE = 4
D_FF = D_MODEL // 2
MAIN_WIDTH = 7680
GATE_ROWS = 32
NEG = -1e30
VMEM_LIMIT = 56 * 1024 * 1024
MOE_TILE = 256
ATT_BLOCK = 512

CB_DA_Q, CB_DA_K, CB_DA_V = 0, 4, 8
CB_ML_Q, CB_ML_K, CB_ML_V, CB_ML_O = 12, 16, 20, 24
CB_GD_Q, CB_GD_K, CB_GD_V, CB_GD_Z = 28, 36, 44, 52


def _params(*sem):
    return pltpu.CompilerParams(dimension_semantics=sem, vmem_limit_bytes=VMEM_LIMIT)


def _dot(a, b):
    return jnp.dot(a, b, preferred_element_type=F32)


def _dot_nt(a, b):
    return lax.dot_general(a, b, (((1,), (1,)), ((), ())), preferred_element_type=F32)


def _dot_tn(a, b):
    return lax.dot_general(a, b, (((0,), (0,)), ((), ())), preferred_element_type=F32)


def _sigmoid(x):
    return 1.0 / (1.0 + jnp.exp(-x))


def _col_from_row(row, n):
    ri = lax.broadcasted_iota(I32, (n, n), 0)
    ci = lax.broadcasted_iota(I32, (n, n), 1)
    return jnp.sum(jnp.where(ri == ci, jnp.broadcast_to(row, (n, n)), 0.0), axis=1, keepdims=True)


def _lane_cumsum(x, seg):
    lane = lax.broadcasted_iota(I32, x.shape, 1) % seg
    s = 1
    while s < seg:
        x = x + jnp.where(lane >= s, pltpu.roll(x, s, 1), 0.0)
        s *= 2
    return x


def _mm_kernel(x_ref, w_ref, o_ref, xb_ref):
    @pl.when(pl.program_id(1) == 0)
    def _():
        xb_ref[...] = x_ref[...].astype(BF16)

    o_ref[...] = _dot(xb_ref[...], w_ref[...]).astype(o_ref.dtype)


def _matmul(x, w, out_dtype, tm, tn):
    m, k = x.shape
    n = w.shape[1]
    tm = min(tm, m)
    return pl.pallas_call(
        _mm_kernel,
        grid=(m // tm, n // tn),
        in_specs=[pl.BlockSpec((tm, k), lambda i, j: (i, 0)),
                  pl.BlockSpec((k, tn), lambda i, j: (0, j))],
        out_specs=pl.BlockSpec((tm, tn), lambda i, j: (i, j)),
        out_shape=jax.ShapeDtypeStruct((m, n), out_dtype),
        scratch_shapes=[pltpu.VMEM((tm, k), BF16)],
        compiler_params=_params("arbitrary", "arbitrary"),
        name="matmul",
    )(x, w)


def _for_blocks(n, fn):
    def pair(jj, carry):
        fn(2 * jj)
        fn(2 * jj + 1)
        return carry

    lax.fori_loop(0, n // 2, pair, 0)

    @pl.when(n % 2 == 1)
    def _():
        fn(n - 1)


def _lane_cummax(x):
    lane = lax.broadcasted_iota(I32, x.shape, 1)
    s = 1
    while s < x.shape[1]:
        x = jnp.maximum(x, jnp.where(lane >= s, pltpu.roll(x, s, 1), NEG))
        s *= 2
    return x


def _gate_kernel(g_ref, c_ref, r_ref, b_ref, rmax_ref, beta_ref, gc_ref):
    g = g_ref[0]
    c = c_ref[...]
    li = g[0:8] + c[0:8]
    xf = g[8:16] + c[8:16]
    lf = jnp.minimum(xf, 0.0) - jnp.log(1.0 + jnp.exp(-jnp.abs(xf)))
    b = _lane_cumsum(lf, lf.shape[1])
    b_ref[0] = b
    r = li - b
    r_ref[0] = r
    rmax_ref[0] = _lane_cummax(r)
    xa = g[16:24] + c[16:24]
    sp = jnp.maximum(xa, 0.0) + jnp.log(1.0 + jnp.exp(-jnp.abs(xa)))
    gdec = -jnp.exp(c[24:32]) * sp
    gc_ref[0] = _lane_cumsum(gdec, CHUNK)
    beta_ref[0] = _sigmoid(g[24:32])


def _gate_prep(gt, consts):
    bsz, _, t = gt.shape
    out = jax.ShapeDtypeStruct((bsz, 8, t), F32)
    spec = pl.BlockSpec((1, 8, t), lambda b: (b, 0, 0))
    return pl.pallas_call(
        _gate_kernel,
        grid=(bsz,),
        in_specs=[pl.BlockSpec((1, GATE_ROWS, t), lambda b: (b, 0, 0)),
                  pl.BlockSpec((GATE_ROWS, 1), lambda b: (0, 0))],
        out_specs=[spec] * 5,
        out_shape=[out] * 5,
        compiler_params=_params("arbitrary"),
        name="gate_prep",
    )(gt, consts)


def _rotary(t, c, s):
    lane = lax.broadcasted_iota(I32, t.shape, 1)
    swapped = jnp.where((lane % DA_QK_DIM) < DA_QK_DIM // 2,
                        pltpu.roll(t, HEAD - DA_QK_DIM // 2, 1), pltpu.roll(t, DA_QK_DIM // 2, 1))
    return t * c + swapped * s


def _da_kernel(q_ref, k_ref, v_ref, cq_ref, sq_ref, ck_ref, sk_ref, lam_ref, g_ref, o_ref,
               kr_ref, vb_ref, s_ref, mx_ref, ls_ref, acc_ref, *, bq, lambda_init):
    qi = pl.program_id(2)
    nh = bq // HEAD

    @pl.when(qi == 0)
    def _():
        kr_ref[...] = _rotary(k_ref[...], ck_ref[...], sk_ref[...]).astype(BF16)
        vb_ref[...] = v_ref[...].astype(BF16)

    qr = _rotary(q_ref[...], cq_ref[...], sq_ref[...]) * (DA_QK_DIM ** -0.5)
    lane = lax.broadcasted_iota(I32, qr.shape, 1)
    qs = jnp.concatenate([jnp.where(lane < DA_QK_DIM, qr, 0.0),
                          jnp.where(lane >= DA_QK_DIM, qr, 0.0)], axis=0).astype(BF16)

    def fold(x, op):
        out = x[:, :HEAD]
        for i in range(1, nh):
            out = op(out, x[:, i * HEAD:(i + 1) * HEAD])
        return out

    mx_ref[...] = jnp.full(mx_ref.shape, NEG, F32)

    def scores(j):
        off = pl.multiple_of(j * bq, bq)
        s = _dot_nt(qs, kr_ref[pl.ds(off, bq), :])
        s_ref[j] = s
        mx_ref[...] = jnp.maximum(mx_ref[...], fold(s, jnp.maximum))

    _for_blocks(qi, scores)
    offd = pl.multiple_of(qi * bq, bq)
    sd = _dot_nt(qs, kr_ref[pl.ds(offd, bq), :])
    r = lax.broadcasted_iota(I32, sd.shape, 0)
    c = lax.broadcasted_iota(I32, sd.shape, 1)
    sd = jnp.where(jnp.where(r >= bq, r - bq, r) >= c, sd, NEG)
    m = jnp.max(jnp.maximum(mx_ref[...], fold(sd, jnp.maximum)), axis=1, keepdims=True)
    mb = jnp.broadcast_to(m, (2 * bq, HEAD))

    ls_ref[...] = jnp.zeros(ls_ref.shape, F32)
    acc_ref[...] = jnp.zeros(acc_ref.shape, F32)

    def accumulate(s, off):
        ps = [jnp.exp(s[:, i * HEAD:(i + 1) * HEAD] - mb) for i in range(nh)]
        tot = ps[0]
        for p in ps[1:]:
            tot = tot + p
        ls_ref[...] += tot
        pb = jnp.concatenate([p.astype(BF16) for p in ps], axis=1)
        acc_ref[...] += _dot(pb, vb_ref[pl.ds(off, bq), :])

    _for_blocks(qi, lambda j: accumulate(s_ref[j], pl.multiple_of(j * bq, bq)))
    accumulate(sd, offd)

    lam = lam_ref[...]
    s1 = jnp.sum(lam[0:1] * lam[1:2], axis=1, keepdims=True)
    s2 = jnp.sum(lam[2:3] * lam[3:4], axis=1, keepdims=True)
    lam_full = jnp.exp(s1) - jnp.exp(s2) + lambda_init
    acc = acc_ref[...]
    l = jnp.sum(ls_ref[...], axis=1, keepdims=True)
    o = acc[:bq] / l[:bq] - lam_full * (acc[bq:] / l[bq:])
    o = o * lax.rsqrt(jnp.mean(o * o, axis=1, keepdims=True) + 1e-6) * g_ref[...]
    o_ref[...] = (o * (1.0 - lambda_init)).astype(o_ref.dtype)


def _diff_attention(proj, cos_t, sin_t, lam, norm_g, lambda_init, bsz, t):
    bq = ATT_BLOCK
    nq = t // bq
    kern = functools.partial(_da_kernel, bq=bq, lambda_init=lambda_init)
    tab_q = pl.BlockSpec((bq, HEAD), lambda b, h, i: (i, 0))
    tab_k = pl.BlockSpec((t, HEAD), lambda b, h, i: (0, 0))
    return pl.pallas_call(
        kern,
        grid=(bsz, DA_HEADS, nq),
        in_specs=[pl.BlockSpec((bq, HEAD), lambda b, h, i: (b * nq + i, CB_DA_Q + h)),
                  pl.BlockSpec((t, HEAD), lambda b, h, i: (b, CB_DA_K + h)),
                  pl.BlockSpec((t, HEAD), lambda b, h, i: (b, CB_DA_V + h)),
                  tab_q, tab_q, tab_k, tab_k,
                  pl.BlockSpec((4, DA_QK_DIM), lambda b, h, i: (0, 0)),
                  pl.BlockSpec((1, HEAD), lambda b, h, i: (0, 0))],
        out_specs=pl.BlockSpec((bq, HEAD), lambda b, h, i: (b * nq + i, h)),
        out_shape=jax.ShapeDtypeStruct((bsz * t, DA_HEADS * HEAD), BF16),
        scratch_shapes=[pltpu.VMEM((t, HEAD), BF16), pltpu.VMEM((t, HEAD), BF16),
                        pltpu.VMEM((nq, 2 * bq, bq), F32),
                        pltpu.VMEM((2 * bq, HEAD), F32), pltpu.VMEM((2 * bq, HEAD), F32),
                        pltpu.VMEM((2 * bq, HEAD), F32)],
        compiler_params=_params("arbitrary", "arbitrary", "arbitrary"),
        name="diff_attention",
    )(proj, proj, proj, cos_t, sin_t, cos_t, sin_t, lam, norm_g)


def _ml_kernel(q_ref, k_ref, v_ref, op_ref, b_ref, r_ref, rmax_ref, g_ref, o_ref,
               kb_ref, vb_ref, den_ref, num_ref, *, bq):
    qi = pl.program_id(2)
    nh = bq // HEAD

    @pl.when(qi == 0)
    def _():
        kb_ref[...] = (k_ref[...] * (HEAD ** -0.5)).astype(BF16)
        vb_ref[...] = v_ref[...].astype(BF16)

    qb = q_ref[...].astype(BF16)
    b_col = _col_from_row(b_ref[0, 0, pl.ds(qi, 1), :], bq)
    rmax_col = _col_from_row(rmax_ref[0, 0, pl.ds(qi, 1), :], bq)
    den_ref[...] = jnp.zeros(den_ref.shape, F32)
    num_ref[...] = jnp.zeros(num_ref.shape, F32)

    def block(j, masked):
        off = pl.multiple_of(j * bq, bq)
        s = _dot_nt(qb, kb_ref[pl.ds(off, bq), :])
        logw = r_ref[0, 0, pl.ds(j, 1), :] - rmax_col
        if masked:
            r = lax.broadcasted_iota(I32, s.shape, 0)
            c = lax.broadcasted_iota(I32, s.shape, 1)
            logw = jnp.where(r >= c, logw, NEG)
        sw = s * jnp.exp(logw)
        tot = sw[:, :HEAD]
        for i in range(1, nh):
            tot = tot + sw[:, i * HEAD:(i + 1) * HEAD]
        den_ref[...] += tot
        num_ref[...] += _dot(sw.astype(BF16), vb_ref[pl.ds(off, bq), :])

    _for_blocks(qi, lambda j: block(j, False))
    block(qi, True)

    den = jnp.sum(den_ref[...], axis=1, keepdims=True)
    h = num_ref[...] / jnp.maximum(jnp.abs(den), jnp.exp(-(b_col + rmax_col)))
    h = h * lax.rsqrt(jnp.mean(h * h, axis=1, keepdims=True) + 1e-6) * g_ref[...]
    o_ref[...] = (_sigmoid(op_ref[...]) * h).astype(o_ref.dtype)


def _mlstm(proj, b_cum, r_gate, r_max, norm_g, bsz, t):
    bq = ATT_BLOCK
    nq = t // bq
    kern = functools.partial(_ml_kernel, bq=bq)
    gate = pl.BlockSpec((1, 1, nq, bq), lambda b, h, i: (b, h, 0, 0))
    return pl.pallas_call(
        kern,
        grid=(bsz, ML_HEADS, nq),
        in_specs=[pl.BlockSpec((bq, HEAD), lambda b, h, i: (b * nq + i, CB_ML_Q + h)),
                  pl.BlockSpec((t, HEAD), lambda b, h, i: (b, CB_ML_K + h)),
                  pl.BlockSpec((t, HEAD), lambda b, h, i: (b, CB_ML_V + h)),
                  pl.BlockSpec((bq, HEAD), lambda b, h, i: (b * nq + i, CB_ML_O + h)),
                  gate, gate, gate,
                  pl.BlockSpec((1, HEAD), lambda b, h, i: (0, 0))],
        out_specs=pl.BlockSpec((bq, HEAD), lambda b, h, i: (b * nq + i, h)),
        out_shape=jax.ShapeDtypeStruct((bsz * t, ML_HEADS * HEAD), BF16),
        scratch_shapes=[pltpu.VMEM((t, HEAD), BF16), pltpu.VMEM((t, HEAD), BF16),
                        pltpu.VMEM((bq, HEAD), F32), pltpu.VMEM((bq, HEAD), F32)],
        compiler_params=_params("arbitrary", "arbitrary", "arbitrary"),
        name="mlstm",
    )(proj, proj, proj, proj, b_cum.reshape(bsz, 8, nq, bq), r_gate.reshape(bsz, 8, nq, bq),
      r_max.reshape(bsz, 8, nq, bq), norm_g)


def _gd_kernel(q_ref, k_ref, v_ref, z_ref, cwq_ref, cwk_ref, cwv_ref, beta_ref, gc_ref, g_ref, o_ref,
               mq_ref, no_ref, egl_ref, *, t, heads):
    L = CHUNK
    S2 = GD_PACK * L
    nc = t // L
    ri = lax.broadcasted_iota(I32, (S2, S2), 0)
    ci = lax.broadcasted_iota(I32, (S2, S2), 1)
    same = (ri // L) == (ci // L)
    tri = same & (ri >= ci)
    strict = same & (ri > ci)
    eye = (ri == ci).astype(F32)
    row_chunk = lax.broadcasted_iota(I32, (S2, 1), 0) // L

    def conv_silu(x_ref, cw_ref, lo, sc, base):
        cur = x_ref[pl.ds(base, S2), lo:lo + HEAD]
        prev = x_ref[pl.ds(pl.multiple_of(jnp.maximum(base - 8, 0), 8), 8), lo:lo + HEAD]
        blk = jnp.concatenate([jnp.where(sc > 0, prev, 0.0), cur], axis=0)
        cw = cw_ref[:, lo:lo + HEAD]
        y = blk[5:5 + S2] * cw[0:1]
        for i in range(1, CONV_W):
            y = y + blk[5 + i:5 + i + S2] * cw[i:i + 1]
        return y * _sigmoid(y)

    def mm(a, b):
        return _dot(a.astype(BF16), b.astype(BF16))

    def local_head(hh, sc):
        lo = hh * HEAD
        base = pl.multiple_of(sc * S2, S2)
        q = conv_silu(q_ref, cwq_ref, lo, sc, base)
        k = conv_silu(k_ref, cwk_ref, lo, sc, base)
        v = conv_silu(v_ref, cwv_ref, lo, sc, base)
        q = q * lax.rsqrt(jnp.sum(q * q, axis=1, keepdims=True) + 1e-6) * (HEAD ** -0.5)
        k = k * lax.rsqrt(jnp.sum(k * k, axis=1, keepdims=True) + 1e-6)
        gc_row = gc_ref[0, hh, pl.ds(sc, 1), :]
        gc_col = _col_from_row(gc_row, S2)
        beta_col = _col_from_row(beta_ref[0, hh, pl.ds(sc, 1), :], S2)
        decay = jnp.exp(jnp.where(tri, gc_col - gc_row, NEG))
        kb = k * beta_col
        kq = _dot_nt(jnp.concatenate([kb, q], axis=0).astype(BF16), k.astype(BF16))
        a = jnp.where(strict, kq[:S2] * decay, 0.0)
        attn = jnp.where(tri, kq[S2:] * decay, 0.0).astype(BF16)
        x = eye - a
        p = mm(a, a)
        for _ in range(4):
            px = mm(jnp.concatenate([p, x], axis=0), p)
            p = px[:S2]
            x = x + px[S2:]
        x = x + mm(x, p)
        rhs = jnp.concatenate([kb * jnp.exp(gc_col), v * beta_col], axis=1)
        wu = mm(x, rhs)
        gls = [gc_row[:, c * L + L - 1:c * L + L] for c in range(GD_PACK)]
        gl_col = gls[-1]
        for c in reversed(range(GD_PACK - 1)):
            gl_col = jnp.where(row_chunk == c, gls[c], gl_col)
        kd = (k * jnp.exp(gl_col - gc_col)).astype(BF16)
        aw = _dot(attn, wu.astype(BF16))
        qprime = (q * jnp.exp(gc_col) - aw[:, :HEAD]).astype(BF16)
        wbd = jnp.concatenate([jnp.where(row_chunk == c, wu, 0.0).astype(BF16) for c in range(GD_PACK)],
                              axis=1)
        mn = _dot_tn(kd, wbd)
        for c in range(GD_PACK):
            rows = slice(c * L, (c + 1) * L)
            blk = pl.multiple_of((GD_PACK * sc + c) * (HEAD + L), L)
            mq_ref[hh, pl.ds(blk, HEAD), :] = mn[:, 2 * c * HEAD:(2 * c + 1) * HEAD].astype(BF16)
            mq_ref[hh, pl.ds(blk + HEAD, L), :] = qprime[rows]
            no_ref[hh, pl.ds(blk, HEAD), :] = mn[:, (2 * c + 1) * HEAD:(2 * c + 2) * HEAD]
            no_ref[hh, pl.ds(blk + HEAD, L), :] = aw[rows, HEAD:]
            egl_ref[hh, pl.ds(GD_PACK * sc + c, 1), :] = jnp.broadcast_to(jnp.exp(gls[c]), (1, HEAD))

    def local(sc, carry):
        for hh in range(heads):
            local_head(hh, sc)
        return carry

    lax.fori_loop(0, nc // GD_PACK, local, 0, unroll=2)

    gn = g_ref[...]

    def step(c, states):
        base = pl.multiple_of(c * L, L)
        blk = pl.multiple_of(c * (HEAD + L), L)
        new = []
        for hh in range(heads):
            lo = hh * HEAD
            s = states[hh]
            ms = _dot(mq_ref[hh, pl.ds(blk, HEAD + L), :], s.astype(BF16))
            new.append(s * egl_ref[hh, pl.ds(c, 1), :] - ms[:HEAD] + no_ref[hh, pl.ds(blk, HEAD), :])
            o = ms[HEAD:] + no_ref[hh, pl.ds(blk + HEAD, L), :]
            o = o * lax.rsqrt(jnp.mean(o * o, axis=1, keepdims=True) + 1e-6) * gn
            z = z_ref[pl.ds(base, L), lo:lo + HEAD]
            o_ref[pl.ds(base, L), lo:lo + HEAD] = (o * (z * _sigmoid(z))).astype(o_ref.dtype)
        return tuple(new)

    lax.fori_loop(0, nc, step, tuple(jnp.zeros((HEAD, HEAD), F32) for _ in range(heads)), unroll=2)


def _gated_deltanet(proj, conv_w, beta, gc, norm_g, bsz, t, heads=2):
    nc = t // CHUNK
    width = heads * HEAD
    kern = functools.partial(_gd_kernel, t=t, heads=heads)

    def slab(cb):
        return pl.BlockSpec((t, width), lambda b, h: (b, cb // heads + h))

    def cw(off):
        return pl.BlockSpec((CONV_W, width), lambda b, h: (0, off // heads + h))

    nsc, srows = nc // GD_PACK, GD_PACK * CHUNK
    gate = pl.BlockSpec((1, heads, nsc, srows), lambda b, h: (b, h, 0, 0))
    return pl.pallas_call(
        kern,
        grid=(bsz, GD_HEADS // heads),
        in_specs=[slab(CB_GD_Q), slab(CB_GD_K), slab(CB_GD_V), slab(CB_GD_Z),
                  cw(0), cw(GD_HEADS), cw(2 * GD_HEADS), gate, gate,
                  pl.BlockSpec((1, HEAD), lambda b, h: (0, 0))],
        out_specs=pl.BlockSpec((t, width), lambda b, h: (b, h)),
        out_shape=jax.ShapeDtypeStruct((bsz * t, GD_HEADS * HEAD), BF16),
        scratch_shapes=[pltpu.VMEM((heads, nc * (HEAD + CHUNK), HEAD), BF16),
                        pltpu.VMEM((heads, nc * (HEAD + CHUNK), HEAD), F32),
                        pltpu.VMEM((heads, nc, HEAD), F32)],
        compiler_params=_params("arbitrary", "arbitrary"),
        name="gated_deltanet",
    )(proj, proj, proj, proj, conv_w, conv_w, conv_w,
      beta.reshape(bsz, 8, nsc, srows), gc.reshape(bsz, 8, nsc, srows), norm_g)


def _layer_norm_rows(z, g, b):
    mu = jnp.mean(z, axis=1, keepdims=True)
    zc = z - mu
    var = jnp.mean(zc * zc, axis=1, keepdims=True)
    return zc * lax.rsqrt(var + 1e-5) * g + b


def _proj_ln_kernel(*refs, nx, alpha, tn):
    x_refs = refs[:nx]
    w_refs = refs[nx:2 * nx]
    h_ref, g_ref, b_ref, o_ref, z_ref = refs[2 * nx:]
    for n in range(0, D_MODEL, tn):
        acc = alpha * h_ref[:, n:n + tn]
        for x_ref, w_ref in zip(x_refs, w_refs):
            acc = acc + _dot(x_ref[...], w_ref[:, n:n + tn])
        z_ref[:, n:n + tn] = acc
    o_ref[...] = _layer_norm_rows(z_ref[...], g_ref[...], b_ref[...])


def _proj_ln(xs, w, h, g, b, alpha, tm=512):
    n_rows = h.shape[0]
    nx = len(xs)
    kern = functools.partial(_proj_ln_kernel, nx=nx, alpha=alpha, tn=512)
    row = lambda i: (i, 0)
    fixed = lambda i: (0, 0)
    w_specs, start = [], 0
    for x in xs:
        width = x.shape[1]
        w_specs.append(pl.BlockSpec((width, D_MODEL), functools.partial(lambda i, blk: (blk, 0),
                                                                         blk=start // width)))
        start += width
    in_specs = ([pl.BlockSpec((tm, x.shape[1]), row) for x in xs]
                + w_specs
                + [pl.BlockSpec((tm, D_MODEL), row),
                   pl.BlockSpec((1, D_MODEL), fixed), pl.BlockSpec((1, D_MODEL), fixed)])
    return pl.pallas_call(
        kern,
        grid=(n_rows // tm,),
        in_specs=in_specs,
        out_specs=pl.BlockSpec((tm, D_MODEL), row),
        out_shape=jax.ShapeDtypeStruct((n_rows, D_MODEL), F32),
        scratch_shapes=[pltpu.VMEM((tm, D_MODEL), F32)],
        compiler_params=_params("arbitrary"),
        name="proj_ln",
    )(*xs, *([w] * nx), h, g, b)


def _xa_kernel(q_ref, kv_ref, o_ref):
    for hh in range(XA_HEADS):
        lo = hh * XA_DIM
        s = _dot_nt(q_ref[:, lo:lo + XA_DIM], kv_ref[:, lo:lo + XA_DIM]) * (XA_DIM ** -0.5)
        s = s - jnp.max(s, axis=1, keepdims=True)
        p = jnp.exp(s)
        p = p / jnp.sum(p, axis=1, keepdims=True)
        o = _dot(p.astype(BF16), kv_ref[:, D_MODEL + lo:D_MODEL + lo + XA_DIM])
        o_ref[:, lo:lo + XA_DIM] = o.astype(o_ref.dtype)


def _cross_attention(q, kv, bsz, t, mem_len, tq=512):
    nq = t // tq
    return pl.pallas_call(
        _xa_kernel,
        grid=(bsz, nq),
        in_specs=[pl.BlockSpec((tq, D_MODEL), lambda b, i: (b * nq + i, 0)),
                  pl.BlockSpec((mem_len, 2 * D_MODEL), lambda b, i: (b, 0))],
        out_specs=pl.BlockSpec((tq, D_MODEL), lambda b, i: (b * nq + i, 0)),
        out_shape=jax.ShapeDtypeStruct((bsz * t, D_MODEL), BF16),
        compiler_params=_params("arbitrary", "arbitrary"),
        name="cross_attention",
    )(q, kv)


def _router_kernel(x_ref, rw_ref, bias_ref, ids_ref, wts_ref, cnt_ref, carry_ref, *, tm):
    i = pl.program_id(0)

    @pl.when(i == 0)
    def _():
        carry_ref[...] = jnp.zeros(carry_ref.shape, F32)

    logits = _dot_nt(rw_ref[...], x_ref[...].astype(BF16))
    scores = _sigmoid(logits)
    sel = scores + bias_ref[...]
    srow = [sel[e:e + 1] for e in range(N_EXPERTS)]
    prow = [scores[e:e + 1] for e in range(N_EXPERTS)]

    gscore = []
    for g in range(N_GROUPS):
        s0, s1, s2, s3 = srow[4 * g:4 * g + 4]
        a, b = jnp.maximum(s0, s1), jnp.minimum(s0, s1)
        c, d = jnp.maximum(s2, s3), jnp.minimum(s2, s3)
        gscore.append(jnp.maximum(a, c) + jnp.maximum(jnp.minimum(a, c), jnp.maximum(b, d)))
    gmax = jnp.maximum(jnp.maximum(gscore[0], gscore[1]), jnp.maximum(gscore[2], gscore[3]))
    gi = jnp.where(gscore[0] == gmax, 0, jnp.where(gscore[1] == gmax, 1,
                                                    jnp.where(gscore[2] == gmax, 2, 3)))

    def pick(rows, j):
        return jnp.where(gi == 0, rows[j], jnp.where(gi == 1, rows[4 + j],
                                                      jnp.where(gi == 2, rows[8 + j], rows[12 + j])))

    v = [pick(srow, j) for j in range(GROUP_SIZE)]
    sc = [pick(prow, j) for j in range(GROUP_SIZE)]
    v1 = jnp.maximum(jnp.maximum(v[0], v[1]), jnp.maximum(v[2], v[3]))
    i1 = jnp.where(v[0] == v1, 0, jnp.where(v[1] == v1, 1, jnp.where(v[2] == v1, 2, 3)))
    rest = [jnp.where(i1 == j, -jnp.inf, v[j]) for j in range(GROUP_SIZE)]
    v2 = jnp.maximum(jnp.maximum(rest[0], rest[1]), jnp.maximum(rest[2], rest[3]))
    i2 = jnp.where(rest[0] == v2, 0, jnp.where(rest[1] == v2, 1, jnp.where(rest[2] == v2, 2, 3)))

    def pick_idx(idx):
        return jnp.where(idx == 0, sc[0], jnp.where(idx == 1, sc[1], jnp.where(idx == 2, sc[2], sc[3])))

    p1, p2 = pick_idx(i1), pick_idx(i2)
    tot = p1 + p2
    e1 = 4 * gi + i1
    e2 = 4 * gi + i2

    erow = lax.broadcasted_iota(I32, (N_EXPERTS, tm), 0)
    hit1 = erow == e1
    hit2 = erow == e2
    onehot = jnp.where(hit1 | hit2, 1.0, 0.0)
    incl = _lane_cumsum(onehot, tm)
    excl = incl - onehot + carry_ref[...]
    rank1 = jnp.sum(jnp.where(hit1, excl, 0.0), axis=0, keepdims=True)
    rank2 = jnp.sum(jnp.where(hit2, excl, 0.0), axis=0, keepdims=True)
    carry_ref[...] = carry_ref[...] + incl[:, tm - 1:tm]

    zi = jnp.zeros((4, tm), I32)
    ids_ref[...] = jnp.concatenate([e1, e2, rank1.astype(I32), rank2.astype(I32), zi], axis=0)
    wts_ref[...] = jnp.concatenate([p1 / tot, p2 / tot, jnp.zeros((6, tm), F32)], axis=0)
    cnt_ref[...] = jnp.broadcast_to(carry_ref[...], cnt_ref.shape)


def _router(h, rw_t, bias_col, tm=2048):
    n = h.shape[0]
    tm = min(tm, n)
    kern = functools.partial(_router_kernel, tm=tm)
    return pl.pallas_call(
        kern,
        grid=(n // tm,),
        in_specs=[pl.BlockSpec((tm, D_MODEL), lambda i: (i, 0)),
                  pl.BlockSpec((N_EXPERTS, D_MODEL), lambda i: (0, 0)),
                  pl.BlockSpec((N_EXPERTS, 1), lambda i: (0, 0))],
        out_specs=[pl.BlockSpec((8, tm), lambda i: (0, i)),
                   pl.BlockSpec((8, tm), lambda i: (0, i)),
                   pl.BlockSpec((N_EXPERTS, HEAD), lambda i: (0, 0))],
        out_shape=[jax.ShapeDtypeStruct((8, n), I32), jax.ShapeDtypeStruct((8, n), F32),
                   jax.ShapeDtypeStruct((N_EXPERTS, HEAD), F32)],
        scratch_shapes=[pltpu.VMEM((N_EXPERTS, 1), F32)],
        compiler_params=_params("arbitrary"),
        name="router",
    )(h, rw_t, bias_col)


def _row_copy(src_hbm, row, dst_ref, slot, sem):
    return pltpu.make_async_copy(src_hbm.at[pl.ds(row, 1), :], dst_ref.at[pl.ds(slot, 1), :], sem)


def _expert_kernel(texp_ref, nused_ref, src_ref, h_hbm, w1_ref, w2_ref, o_ref, xa_ref, xb_ref, sem, *, tm):
    i = pl.program_id(0)
    n_used = nused_ref[0]
    bufs = (xa_ref, xb_ref)

    def wait_rows(buf):
        pltpu.make_async_copy(h_hbm.at[pl.ds(0, tm), :], bufs[buf], sem.at[buf]).wait()

    @pl.when(i == 0)
    def _():
        def body(r, carry):
            _row_copy(h_hbm, src_ref[r], xa_ref, r, sem.at[0]).start()
            return carry

        lax.fori_loop(0, tm, body, 0, unroll=8)

    def tile(cur):
        wait_rows(cur)
        nxt = jnp.minimum(i + 1, n_used - 1) * tm
        for r in range(tm):
            _row_copy(h_hbm, src_ref[nxt + r], bufs[1 - cur], r, sem.at[1 - cur]).start(priority=r % 2)
        mid = _dot(bufs[cur][...].astype(BF16), w1_ref[0])
        gate = mid[:, :D_FF]
        act = (gate * _sigmoid(gate)) * mid[:, D_FF:]
        o_ref[...] = _dot(act.astype(BF16), w2_ref[0])

        @pl.when(i + 1 == n_used)
        def _():
            wait_rows(1 - cur)

    @pl.when((i < n_used) & (i % 2 == 0))
    def _():
        tile(0)

    @pl.when((i < n_used) & (i % 2 == 1))
    def _():
        tile(1)

    @pl.when(i >= n_used)
    def _():
        o_ref[...] = jnp.zeros(o_ref.shape, F32)


def _experts(h, w1, w2, tile_expert, n_used, src_rows, n_tiles):
    tm = MOE_TILE
    kern = functools.partial(_expert_kernel, tm=tm)
    grid_spec = pltpu.PrefetchScalarGridSpec(
        num_scalar_prefetch=3,
        grid=(n_tiles,),
        in_specs=[pl.BlockSpec(memory_space=pl.ANY),
                  pl.BlockSpec((1, D_MODEL, 2 * D_FF), lambda i, te, nu, sr: (te[i], 0, 0)),
                  pl.BlockSpec((1, D_FF, D_MODEL), lambda i, te, nu, sr: (te[i], 0, 0))],
        out_specs=pl.BlockSpec((tm, D_MODEL), lambda i, te, nu, sr: (i, 0)),
        scratch_shapes=[pltpu.VMEM((tm, D_MODEL), F32), pltpu.VMEM((tm, D_MODEL), F32),
                        pltpu.SemaphoreType.DMA((2,))],
    )
    return pl.pallas_call(
        kern,
        grid_spec=grid_spec,
        out_shape=jax.ShapeDtypeStruct((n_tiles * tm, D_MODEL), F32),
        compiler_params=_params("arbitrary"),
        name="experts",
    )(tile_expert, n_used, src_rows, h, w1, w2)


def _combine_kernel(p1_ref, p2_ref, y_hbm, h_ref, w1_ref, w2_ref, g_ref, b_ref, o_ref,
                    a0_ref, a1_ref, c0_ref, c1_ref, sem_a, sem_c, *, tc, alpha):
    i = pl.program_id(0)
    last = pl.num_programs(0) - 1
    a_bufs = (a0_ref, a1_ref)
    c_bufs = (c0_ref, c1_ref)

    def wait_rows(buf):
        pltpu.make_async_copy(y_hbm.at[pl.ds(0, tc), :], a_bufs[buf], sem_a.at[buf]).wait()
        pltpu.make_async_copy(y_hbm.at[pl.ds(0, tc), :], c_bufs[buf], sem_c.at[buf]).wait()

    @pl.when(i == 0)
    def _():
        def body(r, carry):
            _row_copy(y_hbm, p1_ref[r], a0_ref, r, sem_a.at[0]).start()
            _row_copy(y_hbm, p2_ref[r], c0_ref, r, sem_c.at[0]).start()
            return carry

        lax.fori_loop(0, tc, body, 0, unroll=8)

    def step(cur):
        wait_rows(cur)
        nxt = jnp.minimum(i + 1, last) * tc
        for r in range(tc):
            _row_copy(y_hbm, p1_ref[nxt + r], a_bufs[1 - cur], r, sem_a.at[1 - cur]).start(priority=0)
            _row_copy(y_hbm, p2_ref[nxt + r], c_bufs[1 - cur], r, sem_c.at[1 - cur]).start(priority=1)
        z = alpha * h_ref[...] + (w1_ref[...] * a_bufs[cur][...] + w2_ref[...] * c_bufs[cur][...])
        o_ref[...] = _layer_norm_rows(z, g_ref[...], b_ref[...])

        @pl.when(i == last)
        def _():
            wait_rows(1 - cur)

    @pl.when(i % 2 == 0)
    def _():
        step(0)

    @pl.when(i % 2 == 1)
    def _():
        step(1)


def _combine(y, h, pos1, pos2, w1c, w2c, g, b, alpha, tc=256):
    n = h.shape[0]
    kern = functools.partial(_combine_kernel, tc=tc, alpha=alpha)
    row = lambda i, p1, p2: (i, 0)
    fixed = lambda i, p1, p2: (0, 0)
    grid_spec = pltpu.PrefetchScalarGridSpec(
        num_scalar_prefetch=2,
        grid=(n // tc,),
        in_specs=[pl.BlockSpec(memory_space=pl.ANY),
                  pl.BlockSpec((tc, D_MODEL), row),
                  pl.BlockSpec((tc, 1), row), pl.BlockSpec((tc, 1), row),
                  pl.BlockSpec((1, D_MODEL), fixed), pl.BlockSpec((1, D_MODEL), fixed)],
        out_specs=pl.BlockSpec((tc, D_MODEL), row),
        scratch_shapes=[pltpu.VMEM((tc, D_MODEL), F32)] * 4
                       + [pltpu.SemaphoreType.DMA((2,)), pltpu.SemaphoreType.DMA((2,))],
    )
    return pl.pallas_call(
        kern,
        grid_spec=grid_spec,
        out_shape=jax.ShapeDtypeStruct((n, D_MODEL), F32),
        compiler_params=_params("arbitrary"),
        name="moe_combine",
    )(pos1, pos2, y, h, w1c, w2c, g, b)


def _src_rows_kernel(p1_ref, p2_ref, lo_ref, hi_ref, o_ref):
    def clear_range(e, carry):
        def clear(r, c):
            o_ref[r] = 0
            return c

        lax.fori_loop(lo_ref[e], hi_ref[e], clear, 0)
        return carry

    lax.fori_loop(0, lo_ref.shape[0], clear_range, 0)

    def place(tok, carry):
        o_ref[p1_ref[tok]] = tok
        o_ref[p2_ref[tok]] = tok
        return carry

    lax.fori_loop(0, p1_ref.shape[0], place, 0, unroll=8)


def _src_rows(pos1, pos2, pad_lo, pad_hi, total):
    smem = pl.BlockSpec(memory_space=pltpu.SMEM)
    return pl.pallas_call(
        _src_rows_kernel,
        in_specs=[smem, smem, smem, smem],
        out_specs=smem,
        out_shape=jax.ShapeDtypeStruct((total,), I32),
        name="expert_row_sources",
    )(pos1, pos2, pad_lo, pad_hi)


def _moe(h, rw_t, bias_col, w1, w2, g, b, alpha):
    n = h.shape[0]
    tm = MOE_TILE
    n_tiles = (2 * n) // tm + N_EXPERTS
    ids, wts, cnt = _router(h, rw_t, bias_col)
    counts = cnt[:, 0].astype(I32)
    padded = ((counts + tm - 1) // tm) * tm
    ends = jnp.cumsum(padded)
    offs = ends - padded
    pos1 = offs[ids[0]] + ids[2]
    pos2 = offs[ids[1]] + ids[3]
    total = n_tiles * tm
    pad_lo = jnp.concatenate([offs + counts, ends[-1:]]).astype(I32)
    pad_hi = jnp.concatenate([ends, jnp.full((1,), total, I32)]).astype(I32)
    src_rows = _src_rows(pos1, pos2, pad_lo, pad_hi, total)
    tile_start = jnp.arange(n_tiles, dtype=I32) * tm
    tile_expert = jnp.minimum(jnp.sum(tile_start[:, None] >= ends[None, :], axis=1),
                              N_EXPERTS - 1).astype(I32)
    n_used = (ends[-1:] // tm).astype(I32)
    y = _experts(h, w1, w2, tile_expert, n_used, src_rows, n_tiles)
    return _combine(y, h, pos1, pos2, wts[0].reshape(n, 1), wts[1].reshape(n, 1), g, b, alpha)


def _rotary_tables(t):
    inv = 1.0 / (ROPE_THETA ** (jnp.arange(0, DA_QK_DIM, 2, dtype=F32) / DA_QK_DIM))
    ang = jnp.arange(t, dtype=F32)[:, None] * inv[None, :]
    cos, sin = jnp.cos(ang), jnp.sin(ang)
    return jnp.tile(cos, (1, 4)), jnp.concatenate([-sin, sin, -sin, sin], axis=1)


def _split_w_in(w):
    main = jnp.concatenate([w[:, :3584], w[:, 3592:7688]], axis=1).astype(BF16)
    zero4 = jnp.zeros((w.shape[0], 4), w.dtype)
    gate = jnp.concatenate([w[:, 3584:3588], zero4, w[:, 3588:3592], zero4, w[:, 7688:7704],
                            jnp.zeros((w.shape[0], HEAD - GATE_ROWS), w.dtype)], axis=1).astype(BF16)
    return main, gate


def kernel(x, mem, w_in, da_lambda, da_norm_g, ml_gate_b, ml_norm_g, gd_conv_w, gd_a_log, gd_dt_bias,
           gd_norm_g, w_out, xa_wq, xa_wkv, xa_wo, router_w, router_bias, moe_w_in, moe_w_out,
           ln_g, ln_b):
    bsz, t, d = x.shape
    mem_len = mem.shape[1]
    depth = w_in.shape[0]
    n = bsz * t
    alpha = (2 * depth) ** 0.25
    cos_t, sin_t = _rotary_tables(t)
    rw_t = router_w.T.astype(BF16)
    bias_col = router_bias.reshape(N_EXPERTS, 1).astype(F32)
    mem2 = mem.reshape(bsz * mem_len, d)
    zero4 = jnp.zeros((4,), F32)
    h = x.reshape(n, d)
    for l in range(depth):
        lambda_init = 0.8 - 0.6 * math.exp(-0.3 * l)
        w_main, w_gate = _split_w_in(w_in[l])
        proj = _matmul(h, w_main, F32, 1024, 512)
        gates = _matmul(h, w_gate, F32, 1024, HEAD)
        gt = gates[:, :GATE_ROWS].reshape(bsz, t, GATE_ROWS).transpose(0, 2, 1)
        consts = jnp.concatenate([ml_gate_b[l, 0], zero4, ml_gate_b[l, 1], zero4,
                                  gd_dt_bias[l], gd_a_log[l]]).reshape(GATE_ROWS, 1)
        r_gate, b_cum, r_max, beta, gc = _gate_prep(gt, consts)
        y_da = _diff_attention(proj, cos_t, sin_t, da_lambda[l], da_norm_g[l].reshape(1, HEAD),
                               lambda_init, bsz, t)
        y_ml = _mlstm(proj, b_cum, r_gate, r_max, ml_norm_g[l].reshape(1, HEAD), bsz, t)
        y_gd = _gated_deltanet(proj, gd_conv_w[l], beta, gc, gd_norm_g[l].reshape(1, HEAD), bsz, t)
        h = _proj_ln([y_da, y_ml, y_gd], w_out[l].astype(BF16), h,
                     ln_g[l, 0].reshape(1, d), ln_b[l, 0].reshape(1, d), alpha)
        q = _matmul(h, xa_wq[l].astype(BF16), BF16, 1024, 512)
        kv = _matmul(mem2, xa_wkv[l].astype(BF16), BF16, 1024, 512)
        xo = _cross_attention(q, kv, bsz, t, mem_len)
        h = _proj_ln([xo], xa_wo[l].astype(BF16), h,
                     ln_g[l, 1].reshape(1, d), ln_b[l, 1].reshape(1, d), alpha)
        h = _moe(h, rw_t, bias_col, moe_w_in[l].astype(BF16), moe_w_out[l].astype(BF16),
                 ln_g[l, 2].reshape(1, d), ln_b[l, 2].reshape(1, d), alpha)
    return h.reshape(bsz, t, d)
```

```python
import functools
import math

import jax
import jax.numpy as jnp
from jax import lax
from jax.experimental import pallas as pl
from jax.experimental.pallas import tpu as pltpu

F32 = jnp.float32
BF16 = jnp.bfloat16
I32 = jnp.int32

D_MODEL = 2048
HEAD = 128
DA_HEADS = 4
DA_QK_DIM = 64
ML_HEADS = 4
GD_HEADS = 8
CHUNK = 64
GD_PACK = 4
CONV_W = 4
ROPE_THETA = 10000.0
XA_HEADS = 4
XA_DIM = D_MODEL // XA_HEADS
N_EXPERTS = 16
N_GROUPS = 4
GROUP_SIZE = 4
D_FF = D_MODEL // 2
MAIN_WIDTH = 7680
GATE_ROWS = 32
NEG = -1e30
VMEM_LIMIT = 56 * 1024 * 1024
MOE_TILE = 256
ATT_BLOCK = 512

CB_DA_Q, CB_DA_K, CB_DA_V = 0, 4, 8
CB_ML_Q, CB_ML_K, CB_ML_V, CB_ML_O = 12, 16, 20, 24
CB_GD_Q, CB_GD_K, CB_GD_V, CB_GD_Z = 28, 36, 44, 52


def _params(*sem):
    return pltpu.CompilerParams(dimension_semantics=sem, vmem_limit_bytes=VMEM_LIMIT)


def _dot(a, b):
    return jnp.dot(a, b, preferred_element_type=F32)


def _dot_nt(a, b):
    return lax.dot_general(a, b, (((1,), (1,)), ((), ())), preferred_element_type=F32)


def _dot_tn(a, b):
    return lax.dot_general(a, b, (((0,), (0,)), ((), ())), preferred_element_type=F32)


def _sigmoid(x):
    return 1.0 / (1.0 + jnp.exp(-x))


def _col_from_row(row, n):
    ri = lax.broadcasted_iota(I32, (n, n), 0)
    ci = lax.broadcasted_iota(I32, (n, n), 1)
    return jnp.sum(jnp.where(ri == ci, jnp.broadcast_to(row, (n, n)), 0.0), axis=1, keepdims=True)


def _lane_cumsum(x, seg):
    lane = lax.broadcasted_iota(I32, x.shape, 1) % seg
    s = 1
    while s < seg:
        x = x + jnp.where(lane >= s, pltpu.roll(x, s, 1), 0.0)
        s *= 2
    return x


def _mm_kernel(x_ref, w_ref, o_ref, xb_ref):
    @pl.when(pl.program_id(1) == 0)
    def _():
        xb_ref[...] = x_ref[...].astype(BF16)

    o_ref[...] = _dot(xb_ref[...], w_ref[...]).astype(o_ref.dtype)


def _matmul(x, w, out_dtype, tm, tn):
    m, k = x.shape
    n = w.shape[1]
    tm = min(tm, m)
    return pl.pallas_call(
        _mm_kernel,
        grid=(m // tm, n // tn),
        in_specs=[pl.BlockSpec((tm, k), lambda i, j: (i, 0)),
                  pl.BlockSpec((k, tn), lambda i, j: (0, j))],
        out_specs=pl.BlockSpec((tm, tn), lambda i, j: (i, j)),
        out_shape=jax.ShapeDtypeStruct((m, n), out_dtype),
        scratch_shapes=[pltpu.VMEM((tm, k), BF16)],
        compiler_params=_params("arbitrary", "arbitrary"),
        name="matmul",
    )(x, w)


def _for_blocks(n, fn):
    def pair(jj, carry):
        fn(2 * jj)
        fn(2 * jj + 1)
        return carry

    lax.fori_loop(0, n // 2, pair, 0)

    @pl.when(n % 2 == 1)
    def _():
        fn(n - 1)


def _lane_cummax(x):
    lane = lax.broadcasted_iota(I32, x.shape, 1)
    s = 1
    while s < x.shape[1]:
        x = jnp.maximum(x, jnp.where(lane >= s, pltpu.roll(x, s, 1), NEG))
        s *= 2
    return x


def _gate_kernel(g_ref, c_ref, r_ref, b_ref, rmax_ref, beta_ref, gc_ref):
    g = g_ref[0]
    c = c_ref[...]
    li = g[0:8] + c[0:8]
    xf = g[8:16] + c[8:16]
    lf = jnp.minimum(xf, 0.0) - jnp.log(1.0 + jnp.exp(-jnp.abs(xf)))
    b = _lane_cumsum(lf, lf.shape[1])
    b_ref[0] = b
    r = li - b
    r_ref[0] = r
    rmax_ref[0] = _lane_cummax(r)
    xa = g[16:24] + c[16:24]
    sp = jnp.maximum(xa, 0.0) + jnp.log(1.0 + jnp.exp(-jnp.abs(xa)))
    gdec = -jnp.exp(c[24:32]) * sp
    gc_ref[0] = _lane_cumsum(gdec, CHUNK)
    beta_ref[0] = _sigmoid(g[24:32])


def _gate_prep(gt, consts):
    bsz, _, t = gt.shape
    out = jax.ShapeDtypeStruct((bsz, 8, t), F32)
    spec = pl.BlockSpec((1, 8, t), lambda b: (b, 0, 0))
    return pl.pallas_call(
        _gate_kernel,
        grid=(bsz,),
        in_specs=[pl.BlockSpec((1, GATE_ROWS, t), lambda b: (b, 0, 0)),
                  pl.BlockSpec((GATE_ROWS, 1), lambda b: (0, 0))],
        out_specs=[spec] * 5,
        out_shape=[out] * 5,
        compiler_params=_params("arbitrary"),
        name="gate_prep",
    )(gt, consts)


def _rotary(t, c, s):
    lane = lax.broadcasted_iota(I32, t.shape, 1)
    swapped = jnp.where((lane % DA_QK_DIM) < DA_QK_DIM // 2,
                        pltpu.roll(t, HEAD - DA_QK_DIM // 2, 1), pltpu.roll(t, DA_QK_DIM // 2, 1))
    return t * c + swapped * s


def _da_kernel(q_ref, k_ref, v_ref, cq_ref, sq_ref, ck_ref, sk_ref, lam_ref, g_ref, o_ref,
               kr_ref, vb_ref, s_ref, mx_ref, ls_ref, acc_ref, *, bq, lambda_init):
    qi = pl.program_id(2)
    nh = bq // HEAD

    @pl.when(qi == 0)
    def _():
        kr_ref[...] = _rotary(k_ref[...], ck_ref[...], sk_ref[...]).astype(BF16)
        vb_ref[...] = v_ref[...].astype(BF16)

    qr = _rotary(q_ref[...], cq_ref[...], sq_ref[...]) * (DA_QK_DIM ** -0.5)
    lane = lax.broadcasted_iota(I32, qr.shape, 1)
    qs = jnp.concatenate([jnp.where(lane < DA_QK_DIM, qr, 0.0),
                          jnp.where(lane >= DA_QK_DIM, qr, 0.0)], axis=0).astype(BF16)

    def fold(x, op):
        out = x[:, :HEAD]
        for i in range(1, nh):
            out = op(out, x[:, i * HEAD:(i + 1) * HEAD])
        return out

    mx_ref[...] = jnp.full(mx_ref.shape, NEG, F32)

    def scores(j):
        off = pl.multiple_of(j * bq, bq)
        s = _dot_nt(qs, kr_ref[pl.ds(off, bq), :])
        s_ref[j] = s
        mx_ref[...] = jnp.maximum(mx_ref[...], fold(s, jnp.maximum))

    _for_blocks(qi, scores)
    offd = pl.multiple_of(qi * bq, bq)
    sd = _dot_nt(qs, kr_ref[pl.ds(offd, bq), :])
    r = lax.broadcasted_iota(I32, sd.shape, 0)
    c = lax.broadcasted_iota(I32, sd.shape, 1)
    sd = jnp.where(jnp.where(r >= bq, r - bq, r) >= c, sd, NEG)
    m = jnp.max(jnp.maximum(mx_ref[...], fold(sd, jnp.maximum)), axis=1, keepdims=True)
    mb = jnp.broadcast_to(m, (2 * bq, HEAD))

    ls_ref[...] = jnp.zeros(ls_ref.shape, F32)
    acc_ref[...] = jnp.zeros(acc_ref.shape, F32)

    def accumulate(s, off):
        ps = [jnp.exp(s[:, i * HEAD:(i + 1) * HEAD] - mb) for i in range(nh)]
        tot = ps[0]
        for p in ps[1:]:
            tot = tot + p
        ls_ref[...] += tot
        pb = jnp.concatenate([p.astype(BF16) for p in ps], axis=1)
        acc_ref[...] += _dot(pb, vb_ref[pl.ds(off, bq), :])

    _for_blocks(qi, lambda j: accumulate(s_ref[j], pl.multiple_of(j * bq, bq)))
    accumulate(sd, offd)

    lam = lam_ref[...]
    s1 = jnp.sum(lam[0:1] * lam[1:2], axis=1, keepdims=True)
    s2 = jnp.sum(lam[2:3] * lam[3:4], axis=1, keepdims=True)
    lam_full = jnp.exp(s1) - jnp.exp(s2) + lambda_init
    acc = acc_ref[...]
    l = jnp.sum(ls_ref[...], axis=1, keepdims=True)
    o = acc[:bq] / l[:bq] - lam_full * (acc[bq:] / l[bq:])
    o = o * lax.rsqrt(jnp.mean(o * o, axis=1, keepdims=True) + 1e-6) * g_ref[...]
    o_ref[...] = (o * (1.0 - lambda_init)).astype(o_ref.dtype)


def _diff_attention(proj, cos_t, sin_t, lam, norm_g, lambda_init, bsz, t):
    bq = ATT_BLOCK
    nq = t // bq
    kern = functools.partial(_da_kernel, bq=bq, lambda_init=lambda_init)
    tab_q = pl.BlockSpec((bq, HEAD), lambda b, h, i: (i, 0))
    tab_k = pl.BlockSpec((t, HEAD), lambda b, h, i: (0, 0))
    return pl.pallas_call(
        kern,
        grid=(bsz, DA_HEADS, nq),
        in_specs=[pl.BlockSpec((bq, HEAD), lambda b, h, i: (b * nq + i, CB_DA_Q + h)),
                  pl.BlockSpec((t, HEAD), lambda b, h, i: (b, CB_DA_K + h)),
                  pl.BlockSpec((t, HEAD), lambda b, h, i: (b, CB_DA_V + h)),
                  tab_q, tab_q, tab_k, tab_k,
                  pl.BlockSpec((4, DA_QK_DIM), lambda b, h, i: (0, 0)),
                  pl.BlockSpec((1, HEAD), lambda b, h, i: (0, 0))],
        out_specs=pl.BlockSpec((bq, HEAD), lambda b, h, i: (b * nq + i, h)),
        out_shape=jax.ShapeDtypeStruct((bsz * t, DA_HEADS * HEAD), BF16),
        scratch_shapes=[pltpu.VMEM((t, HEAD), BF16), pltpu.VMEM((t, HEAD), BF16),
                        pltpu.VMEM((nq, 2 * bq, bq), F32),
                        pltpu.VMEM((2 * bq, HEAD), F32), pltpu.VMEM((2 * bq, HEAD), F32),
                        pltpu.VMEM((2 * bq, HEAD), F32)],
        compiler_params=_params("arbitrary", "arbitrary", "arbitrary"),
        name="diff_attention",
    )(proj, proj, proj, cos_t, sin_t, cos_t, sin_t, lam, norm_g)


def _ml_kernel(q_ref, k_ref, v_ref, op_ref, b_ref, r_ref, rmax_ref, g_ref, o_ref,
               kb_ref, vb_ref, den_ref, num_ref, *, bq):
    qi = pl.program_id(2)
    nh = bq // HEAD

    @pl.when(qi == 0)
    def _():
        kb_ref[...] = (k_ref[...] * (HEAD ** -0.5)).astype(BF16)
        vb_ref[...] = v_ref[...].astype(BF16)

    qb = q_ref[...].astype(BF16)
    b_col = _col_from_row(b_ref[0, 0, pl.ds(qi, 1), :], bq)
    rmax_col = _col_from_row(rmax_ref[0, 0, pl.ds(qi, 1), :], bq)
    den_ref[...] = jnp.zeros(den_ref.shape, F32)
    num_ref[...] = jnp.zeros(num_ref.shape, F32)

    def block(j, masked):
        off = pl.multiple_of(j * bq, bq)
        s = _dot_nt(qb, kb_ref[pl.ds(off, bq), :])
        logw = r_ref[0, 0, pl.ds(j, 1), :] - rmax_col
        if masked:
            r = lax.broadcasted_iota(I32, s.shape, 0)
            c = lax.broadcasted_iota(I32, s.shape, 1)
            logw = jnp.where(r >= c, logw, NEG)
        sw = s * jnp.exp(logw)
        tot = sw[:, :HEAD]
        for i in range(1, nh):
            tot = tot + sw[:, i * HEAD:(i + 1) * HEAD]
        den_ref[...] += tot
        num_ref[...] += _dot(sw.astype(BF16), vb_ref[pl.ds(off, bq), :])

    _for_blocks(qi, lambda j: block(j, False))
    block(qi, True)

    den = jnp.sum(den_ref[...], axis=1, keepdims=True)
    h = num_ref[...] / jnp.maximum(jnp.abs(den), jnp.exp(-(b_col + rmax_col)))
    h = h * lax.rsqrt(jnp.mean(h * h, axis=1, keepdims=True) + 1e-6) * g_ref[...]
    o_ref[...] = (_sigmoid(op_ref[...]) * h).astype(o_ref.dtype)


def _mlstm(proj, b_cum, r_gate, r_max, norm_g, bsz, t):
    bq = ATT_BLOCK
    nq = t // bq
    kern = functools.partial(_ml_kernel, bq=bq)
    gate = pl.BlockSpec((1, 1, nq, bq), lambda b, h, i: (b, h, 0, 0))
    return pl.pallas_call(
        kern,
        grid=(bsz, ML_HEADS, nq),
        in_specs=[pl.BlockSpec((bq, HEAD), lambda b, h, i: (b * nq + i, CB_ML_Q + h)),
                  pl.BlockSpec((t, HEAD), lambda b, h, i: (b, CB_ML_K + h)),
                  pl.BlockSpec((t, HEAD), lambda b, h, i: (b, CB_ML_V + h)),
                  pl.BlockSpec((bq, HEAD), lambda b, h, i: (b * nq + i, CB_ML_O + h)),
                  gate, gate, gate,
                  pl.BlockSpec((1, HEAD), lambda b, h, i: (0, 0))],
        out_specs=pl.BlockSpec((bq, HEAD), lambda b, h, i: (b * nq + i, h)),
        out_shape=jax.ShapeDtypeStruct((bsz * t, ML_HEADS * HEAD), BF16),
        scratch_shapes=[pltpu.VMEM((t, HEAD), BF16), pltpu.VMEM((t, HEAD), BF16),
                        pltpu.VMEM((bq, HEAD), F32), pltpu.VMEM((bq, HEAD), F32)],
        compiler_params=_params("arbitrary", "arbitrary", "arbitrary"),
        name="mlstm",
    )(proj, proj, proj, proj, b_cum.reshape(bsz, 8, nq, bq), r_gate.reshape(bsz, 8, nq, bq),
      r_max.reshape(bsz, 8, nq, bq), norm_g)


def _gd_kernel(q_ref, k_ref, v_ref, z_ref, cwq_ref, cwk_ref, cwv_ref, beta_ref, gc_ref, g_ref, o_ref,
               mq_ref, no_ref, egl_ref, st_a, st_attn, st_rhs, st_kd, st_qg, *, t, heads):
    L = CHUNK
    S2 = GD_PACK * L
    nc = t // L
    ri = lax.broadcasted_iota(I32, (S2, S2), 0)
    ci = lax.broadcasted_iota(I32, (S2, S2), 1)
    same = (ri // L) == (ci // L)
    tri = same & (ri >= ci)
    strict = same & (ri > ci)
    eye = (ri == ci).astype(F32)
    row_chunk = lax.broadcasted_iota(I32, (S2, 1), 0) // L

    def conv_silu(x_ref, cw_ref, lo, sc, base):
        cur = x_ref[pl.ds(base, S2), lo:lo + HEAD]
        prev = x_ref[pl.ds(pl.multiple_of(jnp.maximum(base - 8, 0), 8), 8), lo:lo + HEAD]
        blk = jnp.concatenate([jnp.where(sc > 0, prev, 0.0), cur], axis=0)
        cw = cw_ref[:, lo:lo + HEAD]
        y = blk[5:5 + S2] * cw[0:1]
        for i in range(1, CONV_W):
            y = y + blk[5 + i:5 + i + S2] * cw[i:i + 1]
        return y * _sigmoid(y)

    def mm(a, b):
        return _dot(a.astype(BF16), b.astype(BF16))

    def prepare(hh, sc, slot):
        lo = hh * HEAD
        base = pl.multiple_of(sc * S2, S2)
        q = conv_silu(q_ref, cwq_ref, lo, sc, base)
        k = conv_silu(k_ref, cwk_ref, lo, sc, base)
        v = conv_silu(v_ref, cwv_ref, lo, sc, base)
        q = q * lax.rsqrt(jnp.sum(q * q, axis=1, keepdims=True) + 1e-6) * (HEAD ** -0.5)
        k = k * lax.rsqrt(jnp.sum(k * k, axis=1, keepdims=True) + 1e-6)
        gc_row = gc_ref[0, hh, pl.ds(sc, 1), :]
        gc_col = _col_from_row(gc_row, S2)
        beta_col = _col_from_row(beta_ref[0, hh, pl.ds(sc, 1), :], S2)
        decay = jnp.exp(jnp.where(tri, gc_col - gc_row, NEG))
        kb = k * beta_col
        kq = _dot_nt(jnp.concatenate([kb, q], axis=0).astype(BF16), k.astype(BF16))
        st_a[slot, hh] = jnp.where(strict, kq[:S2] * decay, 0.0)
        st_attn[slot, hh] = jnp.where(tri, kq[S2:] * decay, 0.0).astype(BF16)
        st_rhs[slot, hh] = jnp.concatenate([kb * jnp.exp(gc_col), v * beta_col], axis=1).astype(BF16)
        gls = [gc_row[:, c * L + L - 1:c * L + L] for c in range(GD_PACK)]
        gl_col = gls[-1]
        for c in reversed(range(GD_PACK - 1)):
            gl_col = jnp.where(row_chunk == c, gls[c], gl_col)
        st_kd[slot, hh] = (k * jnp.exp(gl_col - gc_col)).astype(BF16)
        st_qg[slot, hh] = q * jnp.exp(gc_col)
        for c in range(GD_PACK):
            egl_ref[hh, pl.ds(GD_PACK * sc + c, 1), :] = jnp.broadcast_to(jnp.exp(gls[c]), (1, HEAD))

    def solve(hh, sc, slot):
        a = st_a[slot, hh]
        attn = st_attn[slot, hh]
        x = eye - a
        p = mm(a, a)
        for _ in range(4):
            px = mm(jnp.concatenate([p, x], axis=0), p)
            p = px[:S2]
            x = x + px[S2:]
        x = x + mm(x, p)
        wu = _dot(x.astype(BF16), st_rhs[slot, hh])
        kd = st_kd[slot, hh]
        aw = _dot(attn, wu.astype(BF16))
        qprime = (st_qg[slot, hh] - aw[:, :HEAD]).astype(BF16)
        wbd = jnp.concatenate([jnp.where(row_chunk == c, wu, 0.0).astype(BF16) for c in range(GD_PACK)],
                              axis=1)
        mn = _dot_tn(kd, wbd)
        for c in range(GD_PACK):
            rows = slice(c * L, (c + 1) * L)
            blk = pl.multiple_of((GD_PACK * sc + c) * (HEAD + L), L)
            mq_ref[hh, pl.ds(blk, HEAD), :] = mn[:, 2 * c * HEAD:(2 * c + 1) * HEAD].astype(BF16)
            mq_ref[hh, pl.ds(blk + HEAD, L), :] = qprime[rows]
            no_ref[hh, pl.ds(blk, HEAD), :] = mn[:, (2 * c + 1) * HEAD:(2 * c + 2) * HEAD]
            no_ref[hh, pl.ds(blk + HEAD, L), :] = aw[rows, HEAD:]

    nsc = nc // GD_PACK
    assert nsc % 2 == 0
    def first(sc, carry):
        for hh in range(heads):
            prepare(hh, sc, 0)
        return carry

    lax.fori_loop(0, 1, first, 0)

    def local(j, carry):
        for slot in range(2):
            sc = 2 * j + slot
            for hh in range(heads):
                solve(hh, sc, slot)
            for hh in range(heads):
                prepare(hh, jnp.minimum(sc + 1, nsc - 1), 1 - slot)
        return carry

    lax.fori_loop(0, nsc // 2, local, 0)

    gn = g_ref[...]

    def step(c, states):
        base = pl.multiple_of(c * L, L)
        blk = pl.multiple_of(c * (HEAD + L), L)
        new = []
        for hh in range(heads):
            lo = hh * HEAD
            s = states[hh]
            ms = _dot(mq_ref[hh, pl.ds(blk, HEAD + L), :], s.astype(BF16))
            new.append(s * egl_ref[hh, pl.ds(c, 1), :] - ms[:HEAD] + no_ref[hh, pl.ds(blk, HEAD), :])
            o = ms[HEAD:] + no_ref[hh, pl.ds(blk + HEAD, L), :]
            o = o * lax.rsqrt(jnp.mean(o * o, axis=1, keepdims=True) + 1e-6) * gn
            z = z_ref[pl.ds(base, L), lo:lo + HEAD]
            o_ref[pl.ds(base, L), lo:lo + HEAD] = (o * (z * _sigmoid(z))).astype(o_ref.dtype)
        return tuple(new)

    lax.fori_loop(0, nc, step, tuple(jnp.zeros((HEAD, HEAD), F32) for _ in range(heads)), unroll=2)


def _gated_deltanet(proj, conv_w, beta, gc, norm_g, bsz, t, heads=2):
    nc = t // CHUNK
    width = heads * HEAD
    kern = functools.partial(_gd_kernel, t=t, heads=heads)

    def slab(cb):
        return pl.BlockSpec((t, width), lambda b, h: (b, cb // heads + h))

    def cw(off):
        return pl.BlockSpec((CONV_W, width), lambda b, h: (0, off // heads + h))

    nsc, srows = nc // GD_PACK, GD_PACK * CHUNK
    gate = pl.BlockSpec((1, heads, nsc, srows), lambda b, h: (b, h, 0, 0))
    return pl.pallas_call(
        kern,
        grid=(bsz, GD_HEADS // heads),
        in_specs=[slab(CB_GD_Q), slab(CB_GD_K), slab(CB_GD_V), slab(CB_GD_Z),
                  cw(0), cw(GD_HEADS), cw(2 * GD_HEADS), gate, gate,
                  pl.BlockSpec((1, HEAD), lambda b, h: (0, 0))],
        out_specs=pl.BlockSpec((t, width), lambda b, h: (b, h)),
        out_shape=jax.ShapeDtypeStruct((bsz * t, GD_HEADS * HEAD), BF16),
        scratch_shapes=[pltpu.VMEM((heads, nc * (HEAD + CHUNK), HEAD), BF16),
                        pltpu.VMEM((heads, nc * (HEAD + CHUNK), HEAD), F32),
                        pltpu.VMEM((heads, nc, HEAD), F32),
                        pltpu.VMEM((2, heads, srows, srows), F32),
                        pltpu.VMEM((2, heads, srows, srows), BF16),
                        pltpu.VMEM((2, heads, srows, 2 * HEAD), BF16),
                        pltpu.VMEM((2, heads, srows, HEAD), BF16),
                        pltpu.VMEM((2, heads, srows, HEAD), F32)],
        compiler_params=_params("arbitrary", "arbitrary"),
        name="gated_deltanet",
    )(proj, proj, proj, proj, conv_w, conv_w, conv_w,
      beta.reshape(bsz, 8, nsc, srows), gc.reshape(bsz, 8, nsc, srows), norm_g)


def _layer_norm_rows(z, g, b):
    mu = jnp.mean(z, axis=1, keepdims=True)
    zc = z - mu
    var = jnp.mean(zc * zc, axis=1, keepdims=True)
    return zc * lax.rsqrt(var + 1e-5) * g + b


def _proj_ln_kernel(*refs, nx, alpha, tn):
    x_refs = refs[:nx]
    w_refs = refs[nx:2 * nx]
    h_ref, g_ref, b_ref, o_ref, z_ref = refs[2 * nx:]
    for n in range(0, D_MODEL, tn):
        acc = alpha * h_ref[:, n:n + tn]
        for x_ref, w_ref in zip(x_refs, w_refs):
            acc = acc + _dot(x_ref[...], w_ref[:, n:n + tn])
        z_ref[:, n:n + tn] = acc
    o_ref[...] = _layer_norm_rows(z_ref[...], g_ref[...], b_ref[...])


def _proj_ln(xs, w, h, g, b, alpha, tm=512):
    n_rows = h.shape[0]
    nx = len(xs)
    kern = functools.partial(_proj_ln_kernel, nx=nx, alpha=alpha, tn=512)
    row = lambda i: (i, 0)
    fixed = lambda i: (0, 0)
    w_specs, start = [], 0
    for x in xs:
        width = x.shape[1]
        w_specs.append(pl.BlockSpec((width, D_MODEL), functools.partial(lambda i, blk: (blk, 0),
                                                                         blk=start // width)))
        start += width
    in_specs = ([pl.BlockSpec((tm, x.shape[1]), row) for x in xs]
                + w_specs
                + [pl.BlockSpec((tm, D_MODEL), row),
                   pl.BlockSpec((1, D_MODEL), fixed), pl.BlockSpec((1, D_MODEL), fixed)])
    return pl.pallas_call(
        kern,
        grid=(n_rows // tm,),
        in_specs=in_specs,
        out_specs=pl.BlockSpec((tm, D_MODEL), row),
        out_shape=jax.ShapeDtypeStruct((n_rows, D_MODEL), F32),
        scratch_shapes=[pltpu.VMEM((tm, D_MODEL), F32)],
        compiler_params=_params("arbitrary"),
        name="proj_ln",
    )(*xs, *([w] * nx), h, g, b)


def _xa_kernel(q_ref, kv_ref, o_ref):
    for hh in range(XA_HEADS):
        lo = hh * XA_DIM
        s = _dot_nt(q_ref[:, lo:lo + XA_DIM], kv_ref[:, lo:lo + XA_DIM]) * (XA_DIM ** -0.5)
        s = s - jnp.max(s, axis=1, keepdims=True)
        p = jnp.exp(s)
        p = p / jnp.sum(p, axis=1, keepdims=True)
        o = _dot(p.astype(BF16), kv_ref[:, D_MODEL + lo:D_MODEL + lo + XA_DIM])
        o_ref[:, lo:lo + XA_DIM] = o.astype(o_ref.dtype)


def _cross_attention(q, kv, bsz, t, mem_len, tq=512):
    nq = t // tq
    return pl.pallas_call(
        _xa_kernel,
        grid=(bsz, nq),
        in_specs=[pl.BlockSpec((tq, D_MODEL), lambda b, i: (b * nq + i, 0)),
                  pl.BlockSpec((mem_len, 2 * D_MODEL), lambda b, i: (b, 0))],
        out_specs=pl.BlockSpec((tq, D_MODEL), lambda b, i: (b * nq + i, 0)),
        out_shape=jax.ShapeDtypeStruct((bsz * t, D_MODEL), BF16),
        compiler_params=_params("arbitrary", "arbitrary"),
        name="cross_attention",
    )(q, kv)


def _router_kernel(x_ref, rw_ref, bias_ref, ids_ref, wts_ref, cnt_ref, carry_ref, *, tm):
    i = pl.program_id(0)

    @pl.when(i == 0)
    def _():
        carry_ref[...] = jnp.zeros(carry_ref.shape, F32)

    logits = _dot_nt(rw_ref[...], x_ref[...].astype(BF16))
    scores = _sigmoid(logits)
    sel = scores + bias_ref[...]
    srow = [sel[e:e + 1] for e in range(N_EXPERTS)]
    prow = [scores[e:e + 1] for e in range(N_EXPERTS)]

    gscore = []
    for g in range(N_GROUPS):
        s0, s1, s2, s3 = srow[4 * g:4 * g + 4]
        a, b = jnp.maximum(s0, s1), jnp.minimum(s0, s1)
        c, d = jnp.maximum(s2, s3), jnp.minimum(s2, s3)
        gscore.append(jnp.maximum(a, c) + jnp.maximum(jnp.minimum(a, c), jnp.maximum(b, d)))
    gmax = jnp.maximum(jnp.maximum(gscore[0], gscore[1]), jnp.maximum(gscore[2], gscore[3]))
    gi = jnp.where(gscore[0] == gmax, 0, jnp.where(gscore[1] == gmax, 1,
                                                    jnp.where(gscore[2] == gmax, 2, 3)))

    def pick(rows, j):
        return jnp.where(gi == 0, rows[j], jnp.where(gi == 1, rows[4 + j],
                                                      jnp.where(gi == 2, rows[8 + j], rows[12 + j])))

    v = [pick(srow, j) for j in range(GROUP_SIZE)]
    sc = [pick(prow, j) for j in range(GROUP_SIZE)]
    v1 = jnp.maximum(jnp.maximum(v[0], v[1]), jnp.maximum(v[2], v[3]))
    i1 = jnp.where(v[0] == v1, 0, jnp.where(v[1] == v1, 1, jnp.where(v[2] == v1, 2, 3)))
    rest = [jnp.where(i1 == j, -jnp.inf, v[j]) for j in range(GROUP_SIZE)]
    v2 = jnp.maximum(jnp.maximum(rest[0], rest[1]), jnp.maximum(rest[2], rest[3]))
    i2 = jnp.where(rest[0] == v2, 0, jnp.where(rest[1] == v2, 1, jnp.where(rest[2] == v2, 2, 3)))

    def pick_idx(idx):
        return jnp.where(idx == 0, sc[0], jnp.where(idx == 1, sc[1], jnp.where(idx == 2, sc[2], sc[3])))

    p1, p2 = pick_idx(i1), pick_idx(i2)
    tot = p1 + p2
    e1 = 4 * gi + i1
    e2 = 4 * gi + i2

    erow = lax.broadcasted_iota(I32, (N_EXPERTS, tm), 0)
    hit1 = erow == e1
    hit2 = erow == e2
    onehot = jnp.where(hit1 | hit2, 1.0, 0.0)
    incl = _lane_cumsum(onehot, tm)
    excl = incl - onehot + carry_ref[...]
    rank1 = jnp.sum(jnp.where(hit1, excl, 0.0), axis=0, keepdims=True)
    rank2 = jnp.sum(jnp.where(hit2, excl, 0.0), axis=0, keepdims=True)
    carry_ref[...] = carry_ref[...] + incl[:, tm - 1:tm]

    zi = jnp.zeros((4, tm), I32)
    ids_ref[...] = jnp.concatenate([e1, e2, rank1.astype(I32), rank2.astype(I32), zi], axis=0)
    wts_ref[...] = jnp.concatenate([p1 / tot, p2 / tot, jnp.zeros((6, tm), F32)], axis=0)
    cnt_ref[...] = jnp.broadcast_to(carry_ref[...], cnt_ref.shape)


def _router(h, rw_t, bias_col, tm=2048):
    n = h.shape[0]
    tm = min(tm, n)
    kern = functools.partial(_router_kernel, tm=tm)
    return pl.pallas_call(
        kern,
        grid=(n // tm,),
        in_specs=[pl.BlockSpec((tm, D_MODEL), lambda i: (i, 0)),
                  pl.BlockSpec((N_EXPERTS, D_MODEL), lambda i: (0, 0)),
                  pl.BlockSpec((N_EXPERTS, 1), lambda i: (0, 0))],
        out_specs=[pl.BlockSpec((8, tm), lambda i: (0, i)),
                   pl.BlockSpec((8, tm), lambda i: (0, i)),
                   pl.BlockSpec((N_EXPERTS, HEAD), lambda i: (0, 0))],
        out_shape=[jax.ShapeDtypeStruct((8, n), I32), jax.ShapeDtypeStruct((8, n), F32),
                   jax.ShapeDtypeStruct((N_EXPERTS, HEAD), F32)],
        scratch_shapes=[pltpu.VMEM((N_EXPERTS, 1), F32)],
        compiler_params=_params("arbitrary"),
        name="router",
    )(h, rw_t, bias_col)


def _row_copy(src_hbm, row, dst_ref, slot, sem):
    return pltpu.make_async_copy(src_hbm.at[pl.ds(row, 1), :], dst_ref.at[pl.ds(slot, 1), :], sem)


def _expert_kernel(texp_ref, nused_ref, src_ref, h_hbm, w1_ref, w2_ref, o_ref, xa_ref, xb_ref, xc_ref,
                   sem, *, tm):
    i = pl.program_id(0)
    n_used = nused_ref[0]
    bufs = (xa_ref, xb_ref, xc_ref)
    depth = len(bufs)

    def wait_rows(buf):
        pltpu.make_async_copy(h_hbm.at[pl.ds(0, tm), :], bufs[buf], sem.at[buf]).wait()

    @pl.when(i == 0)
    def _():
        for first in range(depth - 1):
            def body(r, carry, first=first):
                _row_copy(h_hbm, src_ref[first * tm + r], bufs[first], r, sem.at[first]).start()
                return carry

            lax.fori_loop(0, tm, body, 0, unroll=8)

    def tile(cur):
        wait_rows(cur)
        ahead = (cur + depth - 1) % depth
        nxt = jnp.minimum(i + depth - 1, n_used - 1) * tm
        for r in range(tm):
            _row_copy(h_hbm, src_ref[nxt + r], bufs[ahead], r, sem.at[ahead]).start(priority=r % 2)
        mid = _dot(bufs[cur][...].astype(BF16), w1_ref[0])
        gate = mid[:, :D_FF]
        act = (gate * _sigmoid(gate)) * mid[:, D_FF:]
        o_ref[...] = _dot(act.astype(BF16), w2_ref[0])

        @pl.when(i + 1 == n_used)
        def _():
            for k in range(1, depth):
                wait_rows((cur + k) % depth)

    for cur in range(depth):
        @pl.when((i < n_used) & (i % depth == cur))
        def _(cur=cur):
            tile(cur)

    @pl.when(i >= n_used)
    def _():
        o_ref[...] = jnp.zeros(o_ref.shape, F32)


def _experts(h, w1, w2, tile_expert, n_used, src_rows, n_tiles):
    tm = MOE_TILE
    kern = functools.partial(_expert_kernel, tm=tm)
    grid_spec = pltpu.PrefetchScalarGridSpec(
        num_scalar_prefetch=3,
        grid=(n_tiles,),
        in_specs=[pl.BlockSpec(memory_space=pl.ANY),
                  pl.BlockSpec((1, D_MODEL, 2 * D_FF), lambda i, te, nu, sr: (te[i], 0, 0)),
                  pl.BlockSpec((1, D_FF, D_MODEL), lambda i, te, nu, sr: (te[i], 0, 0))],
        out_specs=pl.BlockSpec((tm, D_MODEL), lambda i, te, nu, sr: (i, 0)),
        scratch_shapes=[pltpu.VMEM((tm, D_MODEL), F32)] * 3 + [pltpu.SemaphoreType.DMA((3,))],
    )
    return pl.pallas_call(
        kern,
        grid_spec=grid_spec,
        out_shape=jax.ShapeDtypeStruct((n_tiles * tm, D_MODEL), F32),
        compiler_params=_params("arbitrary"),
        name="experts",
    )(tile_expert, n_used, src_rows, h, w1, w2)


def _combine_kernel(p1_ref, p2_ref, y_hbm, h_ref, w1_ref, w2_ref, g_ref, b_ref, o_ref,
                    a0_ref, a1_ref, c0_ref, c1_ref, sem_a, sem_c, *, tc, alpha):
    i = pl.program_id(0)
    last = pl.num_programs(0) - 1
    a_bufs = (a0_ref, a1_ref)
    c_bufs = (c0_ref, c1_ref)

    def wait_rows(buf):
        pltpu.make_async_copy(y_hbm.at[pl.ds(0, tc), :], a_bufs[buf], sem_a.at[buf]).wait()
        pltpu.make_async_copy(y_hbm.at[pl.ds(0, tc), :], c_bufs[buf], sem_c.at[buf]).wait()

    @pl.when(i == 0)
    def _():
        def body(r, carry):
            _row_copy(y_hbm, p1_ref[r], a0_ref, r, sem_a.at[0]).start()
            _row_copy(y_hbm, p2_ref[r], c0_ref, r, sem_c.at[0]).start()
            return carry

        lax.fori_loop(0, tc, body, 0, unroll=8)

    def step(cur):
        wait_rows(cur)
        nxt = jnp.minimum(i + 1, last) * tc
        for r in range(tc):
            _row_copy(y_hbm, p1_ref[nxt + r], a_bufs[1 - cur], r, sem_a.at[1 - cur]).start(priority=0)
            _row_copy(y_hbm, p2_ref[nxt + r], c_bufs[1 - cur], r, sem_c.at[1 - cur]).start(priority=1)
        z = alpha * h_ref[...] + (w1_ref[...] * a_bufs[cur][...] + w2_ref[...] * c_bufs[cur][...])
        o_ref[...] = _layer_norm_rows(z, g_ref[...], b_ref[...])

        @pl.when(i == last)
        def _():
            wait_rows(1 - cur)

    @pl.when(i % 2 == 0)
    def _():
        step(0)

    @pl.when(i % 2 == 1)
    def _():
        step(1)


def _combine(y, h, pos1, pos2, w1c, w2c, g, b, alpha, tc=256):
    n = h.shape[0]
    kern = functools.partial(_combine_kernel, tc=tc, alpha=alpha)
    row = lambda i, p1, p2: (i, 0)
    fixed = lambda i, p1, p2: (0, 0)
    grid_spec = pltpu.PrefetchScalarGridSpec(
        num_scalar_prefetch=2,
        grid=(n // tc,),
        in_specs=[pl.BlockSpec(memory_space=pl.ANY),
                  pl.BlockSpec((tc, D_MODEL), row),
                  pl.BlockSpec((tc, 1), row), pl.BlockSpec((tc, 1), row),
                  pl.BlockSpec((1, D_MODEL), fixed), pl.BlockSpec((1, D_MODEL), fixed)],
        out_specs=pl.BlockSpec((tc, D_MODEL), row),
        scratch_shapes=[pltpu.VMEM((tc, D_MODEL), F32)] * 4
                       + [pltpu.SemaphoreType.DMA((2,)), pltpu.SemaphoreType.DMA((2,))],
    )
    return pl.pallas_call(
        kern,
        grid_spec=grid_spec,
        out_shape=jax.ShapeDtypeStruct((n, D_MODEL), F32),
        compiler_params=_params("arbitrary"),
        name="moe_combine",
    )(pos1, pos2, y, h, w1c, w2c, g, b)


def _src_rows_kernel(p1_ref, p2_ref, lo_ref, hi_ref, o_ref):
    def clear_range(e, carry):
        def clear(r, c):
            o_ref[r] = 0
            return c

        lax.fori_loop(lo_ref[e], hi_ref[e], clear, 0)
        return carry

    lax.fori_loop(0, lo_ref.shape[0], clear_range, 0)

    def place(tok, carry):
        o_ref[p1_ref[tok]] = tok
        o_ref[p2_ref[tok]] = tok
        return carry

    lax.fori_loop(0, p1_ref.shape[0], place, 0, unroll=8)


def _src_rows(pos1, pos2, pad_lo, pad_hi, total):
    smem = pl.BlockSpec(memory_space=pltpu.SMEM)
    return pl.pallas_call(
        _src_rows_kernel,
        in_specs=[smem, smem, smem, smem],
        out_specs=smem,
        out_shape=jax.ShapeDtypeStruct((total,), I32),
        name="expert_row_sources",
    )(pos1, pos2, pad_lo, pad_hi)


def _moe(h, rw_t, bias_col, w1, w2, g, b, alpha):
    n = h.shape[0]
    tm = MOE_TILE
    n_tiles = (2 * n) // tm + N_EXPERTS
    ids, wts, cnt = _router(h, rw_t, bias_col)
    counts = cnt[:, 0].astype(I32)
    padded = ((counts + tm - 1) // tm) * tm
    ends = jnp.cumsum(padded)
    offs = ends - padded
    pos1 = offs[ids[0]] + ids[2]
    pos2 = offs[ids[1]] + ids[3]
    total = n_tiles * tm
    pad_lo = jnp.concatenate([offs + counts, ends[-1:]]).astype(I32)
    pad_hi = jnp.concatenate([ends, jnp.full((1,), total, I32)]).astype(I32)
    src_rows = _src_rows(pos1, pos2, pad_lo, pad_hi, total)
    tile_start = jnp.arange(n_tiles, dtype=I32) * tm
    tile_expert = jnp.minimum(jnp.sum(tile_start[:, None] >= ends[None, :], axis=1),
                              N_EXPERTS - 1).astype(I32)
    n_used = (ends[-1:] // tm).astype(I32)
    y = _experts(h, w1, w2, tile_expert, n_used, src_rows, n_tiles)
    return _combine(y, h, pos1, pos2, wts[0].reshape(n, 1), wts[1].reshape(n, 1), g, b, alpha)


def _rotary_tables(t):
    inv = 1.0 / (ROPE_THETA ** (jnp.arange(0, DA_QK_DIM, 2, dtype=F32) / DA_QK_DIM))
    ang = jnp.arange(t, dtype=F32)[:, None] * inv[None, :]
    cos, sin = jnp.cos(ang), jnp.sin(ang)
    return jnp.tile(cos, (1, 4)), jnp.concatenate([-sin, sin, -sin, sin], axis=1)


def _split_w_in(w):
    main = jnp.concatenate([w[:, :3584], w[:, 3592:7688]], axis=1).astype(BF16)
    zero4 = jnp.zeros((w.shape[0], 4), w.dtype)
    gate = jnp.concatenate([w[:, 3584:3588], zero4, w[:, 3588:3592], zero4, w[:, 7688:7704],
                            jnp.zeros((w.shape[0], HEAD - GATE_ROWS), w.dtype)], axis=1).astype(BF16)
    return main, gate


def kernel(x, mem, w_in, da_lambda, da_norm_g, ml_gate_b, ml_norm_g, gd_conv_w, gd_a_log, gd_dt_bias,
           gd_norm_g, w_out, xa_wq, xa_wkv, xa_wo, router_w, router_bias, moe_w_in, moe_w_out,
           ln_g, ln_b):
    bsz, t, d = x.shape
    mem_len = mem.shape[1]
    depth = w_in.shape[0]
    n = bsz * t
    alpha = (2 * depth) ** 0.25
    cos_t, sin_t = _rotary_tables(t)
    rw_t = router_w.T.astype(BF16)
    bias_col = router_bias.reshape(N_EXPERTS, 1).astype(F32)
    mem2 = mem.reshape(bsz * mem_len, d)
    zero4 = jnp.zeros((4,), F32)
    h = x.reshape(n, d)
    for l in range(depth):
        lambda_init = 0.8 - 0.6 * math.exp(-0.3 * l)
        w_main, w_gate = _split_w_in(w_in[l])
        proj = _matmul(h, w_main, F32, 1024, 512)
        gates = _matmul(h, w_gate, F32, 1024, HEAD)
        gt = gates[:, :GATE_ROWS].reshape(bsz, t, GATE_ROWS).transpose(0, 2, 1)
        consts = jnp.concatenate([ml_gate_b[l, 0], zero4, ml_gate_b[l, 1], zero4,
                                  gd_dt_bias[l], gd_a_log[l]]).reshape(GATE_ROWS, 1)
        r_gate, b_cum, r_max, beta, gc = _gate_prep(gt, consts)
        y_da = _diff_attention(proj, cos_t, sin_t, da_lambda[l], da_norm_g[l].reshape(1, HEAD),
                               lambda_init, bsz, t)
        y_ml = _mlstm(proj, b_cum, r_gate, r_max, ml_norm_g[l].reshape(1, HEAD), bsz, t)
        y_gd = _gated_deltanet(proj, gd_conv_w[l], beta, gc, gd_norm_g[l].reshape(1, HEAD), bsz, t)
        h = _proj_ln([y_da, y_ml, y_gd], w_out[l].astype(BF16), h,
                     ln_g[l, 0].reshape(1, d), ln_b[l, 0].reshape(1, d), alpha)
        q = _matmul(h, xa_wq[l].astype(BF16), BF16, 1024, 512)
        kv = _matmul(mem2, xa_wkv[l].astype(BF16), BF16, 1024, 512)
        xo = _cross_attention(q, kv, bsz, t, mem_len)
        h = _proj_ln([xo], xa_wo[l].astype(BF16), h,
                     ln_g[l, 1].reshape(1, d), ln_b[l, 1].reshape(1, d), alpha)
        h = _moe(h, rw_t, bias_col, moe_w_in[l].astype(BF16), moe_w_out[l].astype(BF16),
                 ln_g[l, 2].reshape(1, d), ln_b[l, 2].reshape(1, d), alpha)
    return h.reshape(bsz, t, d)
```

```python
import functools
import math

import jax
import jax.numpy as jnp
from jax import lax
from jax.experimental import pallas as pl
from jax.experimental.pallas import tpu as pltpu

F32 = jnp.float32
BF16 = jnp.bfloat16
I32 = jnp.int32

D_MODEL = 2048
HEAD = 128
DA_HEADS = 4
DA_QK_DIM = 64
ML_HEADS = 4
GD_HEADS = 8
CHUNK = 64
GD_PACK = 4
CONV_W = 4
ROPE_THETA = 10000.0
XA_HEADS = 4
XA_DIM = D_MODEL // XA_HEADS
N_EXPERTS = 16
N_GROUPS = 4
GROUP_SIZE = 4
D_FF = D_MODEL // 2
MAIN_WIDTH = 7680
GATE_ROWS = 32
NEG = -1e30
VMEM_LIMIT = 56 * 1024 * 1024
MOE_TILE = 256
ATT_BLOCK = 512

CB_DA_Q, CB_DA_K, CB_DA_V = 0, 4, 8
CB_ML_Q, CB_ML_K, CB_ML_V, CB_ML_O = 12, 16, 20, 24
CB_GD_Q, CB_GD_K, CB_GD_V, CB_GD_Z = 28, 36, 44, 52


def _params(*sem):
    return pltpu.CompilerParams(dimension_semantics=sem, vmem_limit_bytes=VMEM_LIMIT)


def _dot(a, b):
    return jnp.dot(a, b, preferred_element_type=F32)


def _dot_nt(a, b):
    return lax.dot_general(a, b, (((1,), (1,)), ((), ())), preferred_element_type=F32)


def _dot_tn(a, b):
    return lax.dot_general(a, b, (((0,), (0,)), ((), ())), preferred_element_type=F32)


def _sigmoid(x):
    return 1.0 / (1.0 + jnp.exp(-x))


def _col_from_row(row, n):
    ri = lax.broadcasted_iota(I32, (n, n), 0)
    ci = lax.broadcasted_iota(I32, (n, n), 1)
    return jnp.sum(jnp.where(ri == ci, jnp.broadcast_to(row, (n, n)), 0.0), axis=1, keepdims=True)


def _lane_cumsum(x, seg):
    lane = lax.broadcasted_iota(I32, x.shape, 1) % seg
    s = 1
    while s < seg:
        x = x + jnp.where(lane >= s, pltpu.roll(x, s, 1), 0.0)
        s *= 2
    return x


def _mm_kernel(x_ref, w_ref, o_ref, xb_ref):
    @pl.when(pl.program_id(1) == 0)
    def _():
        xb_ref[...] = x_ref[...].astype(BF16)

    o_ref[...] = _dot(xb_ref[...], w_ref[...]).astype(o_ref.dtype)


def _matmul(x, w, out_dtype, tm, tn):
    m, k = x.shape
    n = w.shape[1]
    tm = min(tm, m)
    return pl.pallas_call(
        _mm_kernel,
        grid=(m // tm, n // tn),
        in_specs=[pl.BlockSpec((tm, k), lambda i, j: (i, 0)),
                  pl.BlockSpec((k, tn), lambda i, j: (0, j))],
        out_specs=pl.BlockSpec((tm, tn), lambda i, j: (i, j)),
        out_shape=jax.ShapeDtypeStruct((m, n), out_dtype),
        scratch_shapes=[pltpu.VMEM((tm, k), BF16)],
        compiler_params=_params("arbitrary", "arbitrary"),
        name="matmul",
    )(x, w)


def _for_blocks(n, fn):
    def pair(jj, carry):
        fn(2 * jj)
        fn(2 * jj + 1)
        return carry

    lax.fori_loop(0, n // 2, pair, 0)

    @pl.when(n % 2 == 1)
    def _():
        fn(n - 1)


def _lane_cummax(x):
    lane = lax.broadcasted_iota(I32, x.shape, 1)
    s = 1
    while s < x.shape[1]:
        x = jnp.maximum(x, jnp.where(lane >= s, pltpu.roll(x, s, 1), NEG))
        s *= 2
    return x


def _gate_kernel(g_ref, c_ref, r_ref, b_ref, rmax_ref, beta_ref, gc_ref):
    g = g_ref[0]
    c = c_ref[...]
    li = g[0:8] + c[0:8]
    xf = g[8:16] + c[8:16]
    lf = jnp.minimum(xf, 0.0) - jnp.log(1.0 + jnp.exp(-jnp.abs(xf)))
    b = _lane_cumsum(lf, lf.shape[1])
    b_ref[0] = b
    r = li - b
    r_ref[0] = r
    rmax_ref[0] = _lane_cummax(r)
    xa = g[16:24] + c[16:24]
    sp = jnp.maximum(xa, 0.0) + jnp.log(1.0 + jnp.exp(-jnp.abs(xa)))
    gdec = -jnp.exp(c[24:32]) * sp
    gc_ref[0] = _lane_cumsum(gdec, CHUNK)
    beta_ref[0] = _sigmoid(g[24:32])


def _gate_prep(gt, consts):
    bsz, _, t = gt.shape
    out = jax.ShapeDtypeStruct((bsz, 8, t), F32)
    spec = pl.BlockSpec((1, 8, t), lambda b: (b, 0, 0))
    return pl.pallas_call(
        _gate_kernel,
        grid=(bsz,),
        in_specs=[pl.BlockSpec((1, GATE_ROWS, t), lambda b: (b, 0, 0)),
                  pl.BlockSpec((GATE_ROWS, 1), lambda b: (0, 0))],
        out_specs=[spec] * 5,
        out_shape=[out] * 5,
        compiler_params=_params("arbitrary"),
        name="gate_prep",
    )(gt, consts)


def _rotary(t, c, s):
    lane = lax.broadcasted_iota(I32, t.shape, 1)
    swapped = jnp.where((lane % DA_QK_DIM) < DA_QK_DIM // 2,
                        pltpu.roll(t, HEAD - DA_QK_DIM // 2, 1), pltpu.roll(t, DA_QK_DIM // 2, 1))
    return t * c + swapped * s


def _da_kernel(q_ref, k_ref, v_ref, cq_ref, sq_ref, ck_ref, sk_ref, lam_ref, g_ref, o_ref,
               kr_ref, vb_ref, s_ref, mx_ref, ls_ref, acc_ref, *, bq, lambda_init):
    qi = pl.program_id(2)
    nh = bq // HEAD

    @pl.when(qi == 0)
    def _():
        kr_ref[...] = _rotary(k_ref[...], ck_ref[...], sk_ref[...]).astype(BF16)
        vb_ref[...] = v_ref[...].astype(BF16)

    qr = _rotary(q_ref[...], cq_ref[...], sq_ref[...]) * (DA_QK_DIM ** -0.5)
    lane = lax.broadcasted_iota(I32, qr.shape, 1)
    qs = jnp.concatenate([jnp.where(lane < DA_QK_DIM, qr, 0.0),
                          jnp.where(lane >= DA_QK_DIM, qr, 0.0)], axis=0).astype(BF16)

    def fold(x, op):
        out = x[:, :HEAD]
        for i in range(1, nh):
            out = op(out, x[:, i * HEAD:(i + 1) * HEAD])
        return out

    mx_ref[...] = jnp.full(mx_ref.shape, NEG, F32)

    def scores(j):
        off = pl.multiple_of(j * bq, bq)
        s = _dot_nt(qs, kr_ref[pl.ds(off, bq), :])
        s_ref[j] = s
        mx_ref[...] = jnp.maximum(mx_ref[...], fold(s, jnp.maximum))

    _for_blocks(qi, scores)
    offd = pl.multiple_of(qi * bq, bq)
    sd = _dot_nt(qs, kr_ref[pl.ds(offd, bq), :])
    r = lax.broadcasted_iota(I32, sd.shape, 0)
    c = lax.broadcasted_iota(I32, sd.shape, 1)
    sd = jnp.where(jnp.where(r >= bq, r - bq, r) >= c, sd, NEG)
    m = jnp.max(jnp.maximum(mx_ref[...], fold(sd, jnp.maximum)), axis=1, keepdims=True)
    mb = jnp.broadcast_to(m, (2 * bq, HEAD))

    ls_ref[...] = jnp.zeros(ls_ref.shape, F32)
    acc_ref[...] = jnp.zeros(acc_ref.shape, F32)

    def accumulate(s, off):
        ps = [jnp.exp(s[:, i * HEAD:(i + 1) * HEAD] - mb) for i in range(nh)]
        tot = ps[0]
        for p in ps[1:]:
            tot = tot + p
        ls_ref[...] += tot
        pb = jnp.concatenate([p.astype(BF16) for p in ps], axis=1)
        acc_ref[...] += _dot(pb, vb_ref[pl.ds(off, bq), :])

    _for_blocks(qi, lambda j: accumulate(s_ref[j], pl.multiple_of(j * bq, bq)))
    accumulate(sd, offd)

    lam = lam_ref[...]
    s1 = jnp.sum(lam[0:1] * lam[1:2], axis=1, keepdims=True)
    s2 = jnp.sum(lam[2:3] * lam[3:4], axis=1, keepdims=True)
    lam_full = jnp.exp(s1) - jnp.exp(s2) + lambda_init
    acc = acc_ref[...]
    l = jnp.sum(ls_ref[...], axis=1, keepdims=True)
    o = acc[:bq] / l[:bq] - lam_full * (acc[bq:] / l[bq:])
    o = o * lax.rsqrt(jnp.mean(o * o, axis=1, keepdims=True) + 1e-6) * g_ref[...]
    o_ref[...] = (o * (1.0 - lambda_init)).astype(o_ref.dtype)


def _diff_attention(proj, cos_t, sin_t, lam, norm_g, lambda_init, bsz, t):
    bq = ATT_BLOCK
    nq = t // bq
    kern = functools.partial(_da_kernel, bq=bq, lambda_init=lambda_init)
    tab_q = pl.BlockSpec((bq, HEAD), lambda b, h, i: (i, 0))
    tab_k = pl.BlockSpec((t, HEAD), lambda b, h, i: (0, 0))
    return pl.pallas_call(
        kern,
        grid=(bsz, DA_HEADS, nq),
        in_specs=[pl.BlockSpec((bq, HEAD), lambda b, h, i: (b * nq + i, CB_DA_Q + h)),
                  pl.BlockSpec((t, HEAD), lambda b, h, i: (b, CB_DA_K + h)),
                  pl.BlockSpec((t, HEAD), lambda b, h, i: (b, CB_DA_V + h)),
                  tab_q, tab_q, tab_k, tab_k,
                  pl.BlockSpec((4, DA_QK_DIM), lambda b, h, i: (0, 0)),
                  pl.BlockSpec((1, HEAD), lambda b, h, i: (0, 0))],
        out_specs=pl.BlockSpec((bq, HEAD), lambda b, h, i: (b * nq + i, h)),
        out_shape=jax.ShapeDtypeStruct((bsz * t, DA_HEADS * HEAD), BF16),
        scratch_shapes=[pltpu.VMEM((t, HEAD), BF16), pltpu.VMEM((t, HEAD), BF16),
                        pltpu.VMEM((nq, 2 * bq, bq), F32),
                        pltpu.VMEM((2 * bq, HEAD), F32), pltpu.VMEM((2 * bq, HEAD), F32),
                        pltpu.VMEM((2 * bq, HEAD), F32)],
        compiler_params=_params("arbitrary", "arbitrary", "arbitrary"),
        name="diff_attention",
    )(proj, proj, proj, cos_t, sin_t, cos_t, sin_t, lam, norm_g)


def _ml_kernel(q_ref, k_ref, v_ref, op_ref, b_ref, r_ref, rmax_ref, g_ref, o_ref,
               kb_ref, vb_ref, den_ref, num_ref, *, bq):
    qi = pl.program_id(2)
    nh = bq // HEAD

    @pl.when(qi == 0)
    def _():
        kb_ref[...] = (k_ref[...] * (HEAD ** -0.5)).astype(BF16)
        vb_ref[...] = v_ref[...].astype(BF16)

    qb = q_ref[...].astype(BF16)
    b_col = _col_from_row(b_ref[0, 0, pl.ds(qi, 1), :], bq)
    rmax_col = _col_from_row(rmax_ref[0, 0, pl.ds(qi, 1), :], bq)
    den_ref[...] = jnp.zeros(den_ref.shape, F32)
    num_ref[...] = jnp.zeros(num_ref.shape, F32)

    def block(j, masked):
        off = pl.multiple_of(j * bq, bq)
        s = _dot_nt(qb, kb_ref[pl.ds(off, bq), :])
        logw = r_ref[0, 0, pl.ds(j, 1), :] - rmax_col
        if masked:
            r = lax.broadcasted_iota(I32, s.shape, 0)
            c = lax.broadcasted_iota(I32, s.shape, 1)
            logw = jnp.where(r >= c, logw, NEG)
        sw = s * jnp.exp(logw)
        tot = sw[:, :HEAD]
        for i in range(1, nh):
            tot = tot + sw[:, i * HEAD:(i + 1) * HEAD]
        den_ref[...] += tot
        num_ref[...] += _dot(sw.astype(BF16), vb_ref[pl.ds(off, bq), :])

    _for_blocks(qi, lambda j: block(j, False))
    block(qi, True)

    den = jnp.sum(den_ref[...], axis=1, keepdims=True)
    h = num_ref[...] / jnp.maximum(jnp.abs(den), jnp.exp(-(b_col + rmax_col)))
    h = h * lax.rsqrt(jnp.mean(h * h, axis=1, keepdims=True) + 1e-6) * g_ref[...]
    o_ref[...] = (_sigmoid(op_ref[...]) * h).astype(o_ref.dtype)


def _mlstm(proj, b_cum, r_gate, r_max, norm_g, bsz, t):
    bq = ATT_BLOCK
    nq = t // bq
    kern = functools.partial(_ml_kernel, bq=bq)
    gate = pl.BlockSpec((1, 1, nq, bq), lambda b, h, i: (b, h, 0, 0))
    return pl.pallas_call(
        kern,
        grid=(bsz, ML_HEADS, nq),
        in_specs=[pl.BlockSpec((bq, HEAD), lambda b, h, i: (b * nq + i, CB_ML_Q + h)),
                  pl.BlockSpec((t, HEAD), lambda b, h, i: (b, CB_ML_K + h)),
                  pl.BlockSpec((t, HEAD), lambda b, h, i: (b, CB_ML_V + h)),
                  pl.BlockSpec((bq, HEAD), lambda b, h, i: (b * nq + i, CB_ML_O + h)),
                  gate, gate, gate,
                  pl.BlockSpec((1, HEAD), lambda b, h, i: (0, 0))],
        out_specs=pl.BlockSpec((bq, HEAD), lambda b, h, i: (b * nq + i, h)),
        out_shape=jax.ShapeDtypeStruct((bsz * t, ML_HEADS * HEAD), BF16),
        scratch_shapes=[pltpu.VMEM((t, HEAD), BF16), pltpu.VMEM((t, HEAD), BF16),
                        pltpu.VMEM((bq, HEAD), F32), pltpu.VMEM((bq, HEAD), F32)],
        compiler_params=_params("arbitrary", "arbitrary", "arbitrary"),
        name="mlstm",
    )(proj, proj, proj, proj, b_cum.reshape(bsz, 8, nq, bq), r_gate.reshape(bsz, 8, nq, bq),
      r_max.reshape(bsz, 8, nq, bq), norm_g)


def _gd_kernel(q_ref, k_ref, v_ref, z_ref, cwq_ref, cwk_ref, cwv_ref, beta_ref, gc_ref, g_ref, o_ref,
               mq_ref, no_ref, egl_ref, st_a, st_attn, st_rhs, st_kd, st_qg, *, t, heads):
    L = CHUNK
    S2 = GD_PACK * L
    nc = t // L
    ri = lax.broadcasted_iota(I32, (S2, S2), 0)
    ci = lax.broadcasted_iota(I32, (S2, S2), 1)
    same = (ri // L) == (ci // L)
    tri = same & (ri >= ci)
    strict = same & (ri > ci)
    eye = (ri == ci).astype(F32)
    row_chunk = lax.broadcasted_iota(I32, (S2, 1), 0) // L

    def conv_silu(x_ref, cw_ref, lo, sc, base):
        cur = x_ref[pl.ds(base, S2), lo:lo + HEAD]
        prev = x_ref[pl.ds(pl.multiple_of(jnp.maximum(base - 8, 0), 8), 8), lo:lo + HEAD]
        blk = jnp.concatenate([jnp.where(sc > 0, prev, 0.0), cur], axis=0)
        cw = cw_ref[:, lo:lo + HEAD]
        y = blk[5:5 + S2] * cw[0:1]
        for i in range(1, CONV_W):
            y = y + blk[5 + i:5 + i + S2] * cw[i:i + 1]
        return y * _sigmoid(y)

    def mm(a, b):
        return _dot(a.astype(BF16), b.astype(BF16))

    def prepare(hh, sc, slot):
        lo = hh * HEAD
        base = pl.multiple_of(sc * S2, S2)
        q = conv_silu(q_ref, cwq_ref, lo, sc, base)
        k = conv_silu(k_ref, cwk_ref, lo, sc, base)
        v = conv_silu(v_ref, cwv_ref, lo, sc, base)
        q = q * lax.rsqrt(jnp.sum(q * q, axis=1, keepdims=True) + 1e-6) * (HEAD ** -0.5)
        k = k * lax.rsqrt(jnp.sum(k * k, axis=1, keepdims=True) + 1e-6)
        gc_row = gc_ref[0, hh, pl.ds(sc, 1), :]
        gc_col = _col_from_row(gc_row, S2)
        beta_col = _col_from_row(beta_ref[0, hh, pl.ds(sc, 1), :], S2)
        decay = jnp.exp(jnp.where(tri, gc_col - gc_row, NEG))
        kb = k * beta_col
        kq = _dot_nt(jnp.concatenate([kb, q], axis=0).astype(BF16), k.astype(BF16))
        st_a[slot, hh] = jnp.where(strict, kq[:S2] * decay, 0.0)
        st_attn[slot, hh] = jnp.where(tri, kq[S2:] * decay, 0.0).astype(BF16)
        st_rhs[slot, hh] = jnp.concatenate([kb * jnp.exp(gc_col), v * beta_col], axis=1).astype(BF16)
        gls = [gc_row[:, c * L + L - 1:c * L + L] for c in range(GD_PACK)]
        gl_col = gls[-1]
        for c in reversed(range(GD_PACK - 1)):
            gl_col = jnp.where(row_chunk == c, gls[c], gl_col)
        st_kd[slot, hh] = (k * jnp.exp(gl_col - gc_col)).astype(BF16)
        st_qg[slot, hh] = q * jnp.exp(gc_col)
        for c in range(GD_PACK):
            egl_ref[hh, pl.ds(GD_PACK * sc + c, 1), :] = jnp.broadcast_to(jnp.exp(gls[c]), (1, HEAD))

    def solve(hh, sc, slot):
        a = st_a[slot, hh]
        attn = st_attn[slot, hh]
        x = eye - a
        p = mm(a, a)
        for _ in range(4):
            px = mm(jnp.concatenate([p, x], axis=0), p)
            p = px[:S2]
            x = x + px[S2:]
        x = x + mm(x, p)
        wu = _dot(x.astype(BF16), st_rhs[slot, hh])
        kd = st_kd[slot, hh]
        aw = _dot(attn, wu.astype(BF16))
        qprime = (st_qg[slot, hh] - aw[:, :HEAD]).astype(BF16)
        wbd = jnp.concatenate([jnp.where(row_chunk == c, wu, 0.0).astype(BF16) for c in range(GD_PACK)],
                              axis=1)
        mn = _dot_tn(kd, wbd)
        for c in range(GD_PACK):
            rows = slice(c * L, (c + 1) * L)
            blk = pl.multiple_of((GD_PACK * sc + c) * (HEAD + L), L)
            mq_ref[hh, pl.ds(blk, HEAD), :] = mn[:, 2 * c * HEAD:(2 * c + 1) * HEAD].astype(BF16)
            mq_ref[hh, pl.ds(blk + HEAD, L), :] = qprime[rows]
            no_ref[hh, pl.ds(blk, HEAD), :] = mn[:, (2 * c + 1) * HEAD:(2 * c + 2) * HEAD]
            no_ref[hh, pl.ds(blk + HEAD, L), :] = aw[rows, HEAD:]

    nsc = nc // GD_PACK
    assert nsc % 2 == 0
    def first(sc, carry):
        for hh in range(heads):
            prepare(hh, sc, 0)
        return carry

    lax.fori_loop(0, 1, first, 0)

    def local(j, carry):
        for slot in range(2):
            sc = 2 * j + slot
            for hh in range(heads):
                solve(hh, sc, slot)
            for hh in range(heads):
                prepare(hh, jnp.minimum(sc + 1, nsc - 1), 1 - slot)
        return carry

    lax.fori_loop(0, nsc // 2, local, 0)

    gn = g_ref[...]

    def step(c, states):
        base = pl.multiple_of(c * L, L)
        blk = pl.multiple_of(c * (HEAD + L), L)
        new = []
        for hh in range(heads):
            lo = hh * HEAD
            s = states[hh]
            ms = _dot(mq_ref[hh, pl.ds(blk, HEAD + L), :], s.astype(BF16))
            new.append(s * egl_ref[hh, pl.ds(c, 1), :] - ms[:HEAD] + no_ref[hh, pl.ds(blk, HEAD), :])
            o = ms[HEAD:] + no_ref[hh, pl.ds(blk + HEAD, L), :]
            o = o * lax.rsqrt(jnp.mean(o * o, axis=1, keepdims=True) + 1e-6) * gn
            z = z_ref[pl.ds(base, L), lo:lo + HEAD]
            o_ref[pl.ds(base, L), lo:lo + HEAD] = (o * (z * _sigmoid(z))).astype(o_ref.dtype)
        return tuple(new)

    lax.fori_loop(0, nc, step, tuple(jnp.zeros((HEAD, HEAD), F32) for _ in range(heads)), unroll=2)


def _gated_deltanet(proj, conv_w, beta, gc, norm_g, bsz, t, heads=2):
    nc = t // CHUNK
    width = heads * HEAD
    kern = functools.partial(_gd_kernel, t=t, heads=heads)

    def slab(cb):
        return pl.BlockSpec((t, width), lambda b, h: (b, cb // heads + h))

    def cw(off):
        return pl.BlockSpec((CONV_W, width), lambda b, h: (0, off // heads + h))

    nsc, srows = nc // GD_PACK, GD_PACK * CHUNK
    gate = pl.BlockSpec((1, heads, nsc, srows), lambda b, h: (b, h, 0, 0))
    return pl.pallas_call(
        kern,
        grid=(bsz, GD_HEADS // heads),
        in_specs=[slab(CB_GD_Q), slab(CB_GD_K), slab(CB_GD_V), slab(CB_GD_Z),
                  cw(0), cw(GD_HEADS), cw(2 * GD_HEADS), gate, gate,
                  pl.BlockSpec((1, HEAD), lambda b, h: (0, 0))],
        out_specs=pl.BlockSpec((t, width), lambda b, h: (b, h)),
        out_shape=jax.ShapeDtypeStruct((bsz * t, GD_HEADS * HEAD), BF16),
        scratch_shapes=[pltpu.VMEM((heads, nc * (HEAD + CHUNK), HEAD), BF16),
                        pltpu.VMEM((heads, nc * (HEAD + CHUNK), HEAD), F32),
                        pltpu.VMEM((heads, nc, HEAD), F32),
                        pltpu.VMEM((2, heads, srows, srows), F32),
                        pltpu.VMEM((2, heads, srows, srows), BF16),
                        pltpu.VMEM((2, heads, srows, 2 * HEAD), BF16),
                        pltpu.VMEM((2, heads, srows, HEAD), BF16),
                        pltpu.VMEM((2, heads, srows, HEAD), F32)],
        compiler_params=_params("arbitrary", "arbitrary"),
        name="gated_deltanet",
    )(proj, proj, proj, proj, conv_w, conv_w, conv_w,
      beta.reshape(bsz, 8, nsc, srows), gc.reshape(bsz, 8, nsc, srows), norm_g)


def _layer_norm_rows(z, g, b):
    mu = jnp.mean(z, axis=1, keepdims=True)
    zc = z - mu
    var = jnp.mean(zc * zc, axis=1, keepdims=True)
    return zc * lax.rsqrt(var + 1e-5) * g + b


def _proj_ln_kernel(*refs, nx, alpha, tn):
    x_refs = refs[:nx]
    w_refs = refs[nx:2 * nx]
    h_ref, g_ref, b_ref, o_ref, z_ref = refs[2 * nx:]
    for n in range(0, D_MODEL, tn):
        acc = alpha * h_ref[:, n:n + tn]
        for x_ref, w_ref in zip(x_refs, w_refs):
            acc = acc + _dot(x_ref[...], w_ref[:, n:n + tn])
        z_ref[:, n:n + tn] = acc
    o_ref[...] = _layer_norm_rows(z_ref[...], g_ref[...], b_ref[...])


def _proj_ln(xs, w, h, g, b, alpha, tm=512):
    n_rows = h.shape[0]
    nx = len(xs)
    kern = functools.partial(_proj_ln_kernel, nx=nx, alpha=alpha, tn=512)
    row = lambda i: (i, 0)
    fixed = lambda i: (0, 0)
    w_specs, start = [], 0
    for x in xs:
        width = x.shape[1]
        w_specs.append(pl.BlockSpec((width, D_MODEL), functools.partial(lambda i, blk: (blk, 0),
                                                                         blk=start // width)))
        start += width
    in_specs = ([pl.BlockSpec((tm, x.shape[1]), row) for x in xs]
                + w_specs
                + [pl.BlockSpec((tm, D_MODEL), row),
                   pl.BlockSpec((1, D_MODEL), fixed), pl.BlockSpec((1, D_MODEL), fixed)])
    return pl.pallas_call(
        kern,
        grid=(n_rows // tm,),
        in_specs=in_specs,
        out_specs=pl.BlockSpec((tm, D_MODEL), row),
        out_shape=jax.ShapeDtypeStruct((n_rows, D_MODEL), F32),
        scratch_shapes=[pltpu.VMEM((tm, D_MODEL), F32)],
        compiler_params=_params("arbitrary"),
        name="proj_ln",
    )(*xs, *([w] * nx), h, g, b)


def _xa_kernel(q_ref, kv_ref, o_ref):
    for hh in range(XA_HEADS):
        lo = hh * XA_DIM
        s = _dot_nt(q_ref[:, lo:lo + XA_DIM], kv_ref[:, lo:lo + XA_DIM]) * (XA_DIM ** -0.5)
        s = s - jnp.max(s, axis=1, keepdims=True)
        p = jnp.exp(s)
        p = p / jnp.sum(p, axis=1, keepdims=True)
        o = _dot(p.astype(BF16), kv_ref[:, D_MODEL + lo:D_MODEL + lo + XA_DIM])
        o_ref[:, lo:lo + XA_DIM] = o.astype(o_ref.dtype)


def _cross_attention(q, kv, bsz, t, mem_len, tq=512):
    nq = t // tq
    return pl.pallas_call(
        _xa_kernel,
        grid=(bsz, nq),
        in_specs=[pl.BlockSpec((tq, D_MODEL), lambda b, i: (b * nq + i, 0)),
                  pl.BlockSpec((mem_len, 2 * D_MODEL), lambda b, i: (b, 0))],
        out_specs=pl.BlockSpec((tq, D_MODEL), lambda b, i: (b * nq + i, 0)),
        out_shape=jax.ShapeDtypeStruct((bsz * t, D_MODEL), BF16),
        compiler_params=_params("arbitrary", "arbitrary"),
        name="cross_attention",
    )(q, kv)


def _router_kernel(x_ref, rw_ref, bias_ref, ids_ref, wts_ref, cnt_ref, carry_ref, *, tm):
    i = pl.program_id(0)

    @pl.when(i == 0)
    def _():
        carry_ref[...] = jnp.zeros(carry_ref.shape, F32)

    logits = _dot_nt(rw_ref[...], x_ref[...].astype(BF16))
    scores = _sigmoid(logits)
    sel = scores + bias_ref[...]
    srow = [sel[e:e + 1] for e in range(N_EXPERTS)]
    prow = [scores[e:e + 1] for e in range(N_EXPERTS)]

    gscore = []
    for g in range(N_GROUPS):
        s0, s1, s2, s3 = srow[4 * g:4 * g + 4]
        a, b = jnp.maximum(s0, s1), jnp.minimum(s0, s1)
        c, d = jnp.maximum(s2, s3), jnp.minimum(s2, s3)
        gscore.append(jnp.maximum(a, c) + jnp.maximum(jnp.minimum(a, c), jnp.maximum(b, d)))
    gmax = jnp.maximum(jnp.maximum(gscore[0], gscore[1]), jnp.maximum(gscore[2], gscore[3]))
    gi = jnp.where(gscore[0] == gmax, 0, jnp.where(gscore[1] == gmax, 1,
                                                    jnp.where(gscore[2] == gmax, 2, 3)))

    def pick(rows, j):
        return jnp.where(gi == 0, rows[j], jnp.where(gi == 1, rows[4 + j],
                                                      jnp.where(gi == 2, rows[8 + j], rows[12 + j])))

    v = [pick(srow, j) for j in range(GROUP_SIZE)]
    sc = [pick(prow, j) for j in range(GROUP_SIZE)]
    v1 = jnp.maximum(jnp.maximum(v[0], v[1]), jnp.maximum(v[2], v[3]))
    i1 = jnp.where(v[0] == v1, 0, jnp.where(v[1] == v1, 1, jnp.where(v[2] == v1, 2, 3)))
    rest = [jnp.where(i1 == j, -jnp.inf, v[j]) for j in range(GROUP_SIZE)]
    v2 = jnp.maximum(jnp.maximum(rest[0], rest[1]), jnp.maximum(rest[2], rest[3]))
    i2 = jnp.where(rest[0] == v2, 0, jnp.where(rest[1] == v2, 1, jnp.where(rest[2] == v2, 2, 3)))

    def pick_idx(idx):
        return jnp.where(idx == 0, sc[0], jnp.where(idx == 1, sc[1], jnp.where(idx == 2, sc[2], sc[3])))

    p1, p2 = pick_idx(i1), pick_idx(i2)
    tot = p1 + p2
    e1 = 4 * gi + i1
    e2 = 4 * gi + i2

    erow = lax.broadcasted_iota(I32, (N_EXPERTS, tm), 0)
    hit1 = erow == e1
    hit2 = erow == e2
    onehot = jnp.where(hit1 | hit2, 1.0, 0.0)
    incl = _lane_cumsum(onehot, tm)
    excl = incl - onehot + carry_ref[...]
    rank1 = jnp.sum(jnp.where(hit1, excl, 0.0), axis=0, keepdims=True)
    rank2 = jnp.sum(jnp.where(hit2, excl, 0.0), axis=0, keepdims=True)
    carry_ref[...] = carry_ref[...] + incl[:, tm - 1:tm]

    zi = jnp.zeros((4, tm), I32)
    ids_ref[...] = jnp.concatenate([e1, e2, rank1.astype(I32), rank2.astype(I32), zi], axis=0)
    wts_ref[...] = jnp.concatenate([p1 / tot, p2 / tot, jnp.zeros((6, tm), F32)], axis=0)
    cnt_ref[...] = jnp.broadcast_to(carry_ref[...], cnt_ref.shape)


def _router(h, rw_t, bias_col, tm=2048):
    n = h.shape[0]
    tm = min(tm, n)
    kern = functools.partial(_router_kernel, tm=tm)
    return pl.pallas_call(
        kern,
        grid=(n // tm,),
        in_specs=[pl.BlockSpec((tm, D_MODEL), lambda i: (i, 0)),
                  pl.BlockSpec((N_EXPERTS, D_MODEL), lambda i: (0, 0)),
                  pl.BlockSpec((N_EXPERTS, 1), lambda i: (0, 0))],
        out_specs=[pl.BlockSpec((8, tm), lambda i: (0, i)),
                   pl.BlockSpec((8, tm), lambda i: (0, i)),
                   pl.BlockSpec((N_EXPERTS, HEAD), lambda i: (0, 0))],
        out_shape=[jax.ShapeDtypeStruct((8, n), I32), jax.ShapeDtypeStruct((8, n), F32),
                   jax.ShapeDtypeStruct((N_EXPERTS, HEAD), F32)],
        scratch_shapes=[pltpu.VMEM((N_EXPERTS, 1), F32)],
        compiler_params=_params("arbitrary"),
        name="router",
    )(h, rw_t, bias_col)


def _row_copy(src_hbm, row, dst_ref, slot, sem):
    return pltpu.make_async_copy(src_hbm.at[pl.ds(row, 1), :], dst_ref.at[pl.ds(slot, 1), :], sem)


def _expert_kernel(texp_ref, nused_ref, src_ref, h_hbm, w1_ref, w2_ref, o_ref, xa_ref, xb_ref, xc_ref,
                   xd_ref, sem, *, tm):
    i = pl.program_id(0)
    n_used = nused_ref[0]
    bufs = (xa_ref, xb_ref, xc_ref, xd_ref)
    depth = len(bufs)

    def wait_rows(buf):
        pltpu.make_async_copy(h_hbm.at[pl.ds(0, tm), :], bufs[buf], sem.at[buf]).wait()

    @pl.when(i == 0)
    def _():
        for first in range(depth - 1):
            def body(r, carry, first=first):
                _row_copy(h_hbm, src_ref[first * tm + r], bufs[first], r, sem.at[first]).start()
                return carry

            lax.fori_loop(0, tm, body, 0, unroll=8)

    def tile(cur):
        wait_rows(cur)
        ahead = (cur + depth - 1) % depth
        nxt = jnp.minimum(i + depth - 1, n_used - 1) * tm
        for r in range(tm):
            _row_copy(h_hbm, src_ref[nxt + r], bufs[ahead], r, sem.at[ahead]).start(priority=r % 2)
        mid = _dot(bufs[cur][...].astype(BF16), w1_ref[0])
        gate = mid[:, :D_FF]
        act = (gate * _sigmoid(gate)) * mid[:, D_FF:]
        o_ref[...] = _dot(act.astype(BF16), w2_ref[0])

        @pl.when(i + 1 == n_used)
        def _():
            for k in range(1, depth):
                wait_rows((cur + k) % depth)

    for cur in range(depth):
        @pl.when((i < n_used) & (i % depth == cur))
        def _(cur=cur):
            tile(cur)

    @pl.when(i >= n_used)
    def _():
        o_ref[...] = jnp.zeros(o_ref.shape, F32)


def _experts(h, w1, w2, tile_expert, n_used, src_rows, n_tiles):
    tm = MOE_TILE
    kern = functools.partial(_expert_kernel, tm=tm)
    grid_spec = pltpu.PrefetchScalarGridSpec(
        num_scalar_prefetch=3,
        grid=(n_tiles,),
        in_specs=[pl.BlockSpec(memory_space=pl.ANY),
                  pl.BlockSpec((1, D_MODEL, 2 * D_FF), lambda i, te, nu, sr: (te[i], 0, 0)),
                  pl.BlockSpec((1, D_FF, D_MODEL), lambda i, te, nu, sr: (te[i], 0, 0))],
        out_specs=pl.BlockSpec((tm, D_MODEL), lambda i, te, nu, sr: (i, 0)),
        scratch_shapes=[pltpu.VMEM((tm, D_MODEL), F32)] * 4 + [pltpu.SemaphoreType.DMA((4,))],
    )
    return pl.pallas_call(
        kern,
        grid_spec=grid_spec,
        out_shape=jax.ShapeDtypeStruct((n_tiles * tm, D_MODEL), F32),
        compiler_params=_params("arbitrary"),
        name="experts",
    )(tile_expert, n_used, src_rows, h, w1, w2)


def _combine_kernel(p1_ref, p2_ref, y_hbm, h_ref, w1_ref, w2_ref, g_ref, b_ref, o_ref,
                    a0_ref, a1_ref, a2_ref, c0_ref, c1_ref, c2_ref, sem_a, sem_c, *, tc, alpha):
    i = pl.program_id(0)
    last = pl.num_programs(0) - 1
    a_bufs = (a0_ref, a1_ref, a2_ref)
    c_bufs = (c0_ref, c1_ref, c2_ref)
    depth = len(a_bufs)

    def wait_rows(buf):
        pltpu.make_async_copy(y_hbm.at[pl.ds(0, tc), :], a_bufs[buf], sem_a.at[buf]).wait()
        pltpu.make_async_copy(y_hbm.at[pl.ds(0, tc), :], c_bufs[buf], sem_c.at[buf]).wait()

    @pl.when(i == 0)
    def _():
        for first in range(depth - 1):
            def body(r, carry, first=first):
                _row_copy(y_hbm, p1_ref[first * tc + r], a_bufs[first], r, sem_a.at[first]).start()
                _row_copy(y_hbm, p2_ref[first * tc + r], c_bufs[first], r, sem_c.at[first]).start()
                return carry

            lax.fori_loop(0, tc, body, 0, unroll=8)

    def step(cur):
        wait_rows(cur)
        ahead = (cur + depth - 1) % depth
        nxt = jnp.minimum(i + depth - 1, last) * tc
        for r in range(tc):
            _row_copy(y_hbm, p1_ref[nxt + r], a_bufs[ahead], r, sem_a.at[ahead]).start(priority=0)
            _row_copy(y_hbm, p2_ref[nxt + r], c_bufs[ahead], r, sem_c.at[ahead]).start(priority=1)
        z = alpha * h_ref[...] + (w1_ref[...] * a_bufs[cur][...] + w2_ref[...] * c_bufs[cur][...])
        o_ref[...] = _layer_norm_rows(z, g_ref[...], b_ref[...])

        @pl.when(i == last)
        def _():
            for k in range(1, depth):
                wait_rows((cur + k) % depth)

    for cur in range(depth):
        @pl.when(i % depth == cur)
        def _(cur=cur):
            step(cur)


def _combine(y, h, pos1, pos2, w1c, w2c, g, b, alpha, tc=256):
    n = h.shape[0]
    kern = functools.partial(_combine_kernel, tc=tc, alpha=alpha)
    row = lambda i, p1, p2: (i, 0)
    fixed = lambda i, p1, p2: (0, 0)
    grid_spec = pltpu.PrefetchScalarGridSpec(
        num_scalar_prefetch=2,
        grid=(n // tc,),
        in_specs=[pl.BlockSpec(memory_space=pl.ANY),
                  pl.BlockSpec((tc, D_MODEL), row),
                  pl.BlockSpec((tc, 1), row), pl.BlockSpec((tc, 1), row),
                  pl.BlockSpec((1, D_MODEL), fixed), pl.BlockSpec((1, D_MODEL), fixed)],
        out_specs=pl.BlockSpec((tc, D_MODEL), row),
        scratch_shapes=[pltpu.VMEM((tc, D_MODEL), F32)] * 6
                       + [pltpu.SemaphoreType.DMA((3,)), pltpu.SemaphoreType.DMA((3,))],
    )
    return pl.pallas_call(
        kern,
        grid_spec=grid_spec,
        out_shape=jax.ShapeDtypeStruct((n, D_MODEL), F32),
        compiler_params=_params("arbitrary"),
        name="moe_combine",
    )(pos1, pos2, y, h, w1c, w2c, g, b)


def _src_rows_kernel(p1_ref, p2_ref, lo_ref, hi_ref, o_ref):
    def clear_range(e, carry):
        def clear(r, c):
            o_ref[r] = 0
            return c

        lax.fori_loop(lo_ref[e], hi_ref[e], clear, 0)
        return carry

    lax.fori_loop(0, lo_ref.shape[0], clear_range, 0)

    def place(tok, carry):
        o_ref[p1_ref[tok]] = tok
        o_ref[p2_ref[tok]] = tok
        return carry

    lax.fori_loop(0, p1_ref.shape[0], place, 0, unroll=8)


def _src_rows(pos1, pos2, pad_lo, pad_hi, total):
    smem = pl.BlockSpec(memory_space=pltpu.SMEM)
    return pl.pallas_call(
        _src_rows_kernel,
        in_specs=[smem, smem, smem, smem],
        out_specs=smem,
        out_shape=jax.ShapeDtypeStruct((total,), I32),
        name="expert_row_sources",
    )(pos1, pos2, pad_lo, pad_hi)


def _moe(h, rw_t, bias_col, w1, w2, g, b, alpha):
    n = h.shape[0]
    tm = MOE_TILE
    n_tiles = (2 * n) // tm + N_EXPERTS
    ids, wts, cnt = _router(h, rw_t, bias_col)
    counts = cnt[:, 0].astype(I32)
    padded = ((counts + tm - 1) // tm) * tm
    ends = jnp.cumsum(padded)
    offs = ends - padded
    pos1 = offs[ids[0]] + ids[2]
    pos2 = offs[ids[1]] + ids[3]
    total = n_tiles * tm
    pad_lo = jnp.concatenate([offs + counts, ends[-1:]]).astype(I32)
    pad_hi = jnp.concatenate([ends, jnp.full((1,), total, I32)]).astype(I32)
    src_rows = _src_rows(pos1, pos2, pad_lo, pad_hi, total)
    tile_start = jnp.arange(n_tiles, dtype=I32) * tm
    tile_expert = jnp.minimum(jnp.sum(tile_start[:, None] >= ends[None, :], axis=1),
                              N_EXPERTS - 1).astype(I32)
    n_used = (ends[-1:] // tm).astype(I32)
    y = _experts(h, w1, w2, tile_expert, n_used, src_rows, n_tiles)
    return _combine(y, h, pos1, pos2, wts[0].reshape(n, 1), wts[1].reshape(n, 1), g, b, alpha)


def _rotary_tables(t):
    inv = 1.0 / (ROPE_THETA ** (jnp.arange(0, DA_QK_DIM, 2, dtype=F32) / DA_QK_DIM))
    ang = jnp.arange(t, dtype=F32)[:, None] * inv[None, :]
    cos, sin = jnp.cos(ang), jnp.sin(ang)
    return jnp.tile(cos, (1, 4)), jnp.concatenate([-sin, sin, -sin, sin], axis=1)


def _split_w_in(w):
    main = jnp.concatenate([w[:, :3584], w[:, 3592:7688]], axis=1).astype(BF16)
    zero4 = jnp.zeros((w.shape[0], 4), w.dtype)
    gate = jnp.concatenate([w[:, 3584:3588], zero4, w[:, 3588:3592], zero4, w[:, 7688:7704],
                            jnp.zeros((w.shape[0], HEAD - GATE_ROWS), w.dtype)], axis=1).astype(BF16)
    return main, gate


def kernel(x, mem, w_in, da_lambda, da_norm_g, ml_gate_b, ml_norm_g, gd_conv_w, gd_a_log, gd_dt_bias,
           gd_norm_g, w_out, xa_wq, xa_wkv, xa_wo, router_w, router_bias, moe_w_in, moe_w_out,
           ln_g, ln_b):
    bsz, t, d = x.shape
    mem_len = mem.shape[1]
    depth = w_in.shape[0]
    n = bsz * t
    alpha = (2 * depth) ** 0.25
    cos_t, sin_t = _rotary_tables(t)
    rw_t = router_w.T.astype(BF16)
    bias_col = router_bias.reshape(N_EXPERTS, 1).astype(F32)
    mem2 = mem.reshape(bsz * mem_len, d)
    zero4 = jnp.zeros((4,), F32)
    h = x.reshape(n, d)
    for l in range(depth):
        lambda_init = 0.8 - 0.6 * math.exp(-0.3 * l)
        w_main, w_gate = _split_w_in(w_in[l])
        proj = _matmul(h, w_main, F32, 1024, 512)
        gates = _matmul(h, w_gate, F32, 1024, HEAD)
        gt = gates[:, :GATE_ROWS].reshape(bsz, t, GATE_ROWS).transpose(0, 2, 1)
        consts = jnp.concatenate([ml_gate_b[l, 0], zero4, ml_gate_b[l, 1], zero4,
                                  gd_dt_bias[l], gd_a_log[l]]).reshape(GATE_ROWS, 1)
        r_gate, b_cum, r_max, beta, gc = _gate_prep(gt, consts)
        y_da = _diff_attention(proj, cos_t, sin_t, da_lambda[l], da_norm_g[l].reshape(1, HEAD),
                               lambda_init, bsz, t)
        y_ml = _mlstm(proj, b_cum, r_gate, r_max, ml_norm_g[l].reshape(1, HEAD), bsz, t)
        y_gd = _gated_deltanet(proj, gd_conv_w[l], beta, gc, gd_norm_g[l].reshape(1, HEAD), bsz, t)
        h = _proj_ln([y_da, y_ml, y_gd], w_out[l].astype(BF16), h,
                     ln_g[l, 0].reshape(1, d), ln_b[l, 0].reshape(1, d), alpha)
        q = _matmul(h, xa_wq[l].astype(BF16), BF16, 1024, 512)
        kv = _matmul(mem2, xa_wkv[l].astype(BF16), BF16, 1024, 512)
        xo = _cross_attention(q, kv, bsz, t, mem_len)
        h = _proj_ln([xo], xa_wo[l].astype(BF16), h,
                     ln_g[l, 1].reshape(1, d), ln_b[l, 1].reshape(1, d), alpha)
        h = _moe(h, rw_t, bias_col, moe_w_in[l].astype(BF16), moe_w_out[l].astype(BF16),
                 ln_g[l, 2].reshape(1, d), ln_b[l, 2].reshape(1, d), alpha)
    return h.reshape(bsz, t, d)
```

```python
import functools
import math

import jax
import jax.numpy as jnp
from jax import lax
from jax.experimental import pallas as pl
from jax.experimental.pallas import tpu as pltpu

F32 = jnp.float32
BF16 = jnp.bfloat16
I32 = jnp.int32

D_MODEL = 2048
HEAD = 128
DA_HEADS = 4
DA_QK_DIM = 64
ML_HEADS = 4
GD_HEADS = 8
CHUNK = 64
GD_PACK = 4
CONV_W = 4
ROPE_THETA = 10000.0
XA_HEADS = 4
XA_DIM = D_MODEL // XA_HEADS
N_EXPERTS = 16
N_GROUPS = 4
GROUP_SIZE = 4
D_FF = D_MODEL // 2
MAIN_WIDTH = 7680
GATE_ROWS = 32
NEG = -1e30
VMEM_LIMIT = 56 * 1024 * 1024
MOE_TILE = 256
MOE_ROW_BUFFERS = 6
ATT_BLOCK = 512

CB_DA_Q, CB_DA_K, CB_DA_V = 0, 4, 8
CB_ML_Q, CB_ML_K, CB_ML_V, CB_ML_O = 12, 16, 20, 24
CB_GD_Q, CB_GD_K, CB_GD_V, CB_GD_Z = 28, 36, 44, 52


def _params(*sem):
    return pltpu.CompilerParams(dimension_semantics=sem, vmem_limit_bytes=VMEM_LIMIT)


def _dot(a, b):
    return jnp.dot(a, b, preferred_element_type=F32)


def _dot_nt(a, b):
    return lax.dot_general(a, b, (((1,), (1,)), ((), ())), preferred_element_type=F32)


def _dot_tn(a, b):
    return lax.dot_general(a, b, (((0,), (0,)), ((), ())), preferred_element_type=F32)


def _sigmoid(x):
    return 1.0 / (1.0 + jnp.exp(-x))


def _col_from_row(row, n):
    ri = lax.broadcasted_iota(I32, (n, n), 0)
    ci = lax.broadcasted_iota(I32, (n, n), 1)
    return jnp.sum(jnp.where(ri == ci, jnp.broadcast_to(row, (n, n)), 0.0), axis=1, keepdims=True)


def _lane_cumsum(x, seg):
    lane = lax.broadcasted_iota(I32, x.shape, 1) % seg
    s = 1
    while s < seg:
        x = x + jnp.where(lane >= s, pltpu.roll(x, s, 1), 0.0)
        s *= 2
    return x


def _mm_kernel(x_ref, w_ref, o_ref, xb_ref):
    @pl.when(pl.program_id(1) == 0)
    def _():
        xb_ref[...] = x_ref[...].astype(BF16)

    o_ref[...] = _dot(xb_ref[...], w_ref[...]).astype(o_ref.dtype)


def _matmul(x, w, out_dtype, tm, tn):
    m, k = x.shape
    n = w.shape[1]
    tm = min(tm, m)
    return pl.pallas_call(
        _mm_kernel,
        grid=(m // tm, n // tn),
        in_specs=[pl.BlockSpec((tm, k), lambda i, j: (i, 0)),
                  pl.BlockSpec((k, tn), lambda i, j: (0, j))],
        out_specs=pl.BlockSpec((tm, tn), lambda i, j: (i, j)),
        out_shape=jax.ShapeDtypeStruct((m, n), out_dtype),
        scratch_shapes=[pltpu.VMEM((tm, k), BF16)],
        compiler_params=_params("arbitrary", "arbitrary"),
        name="matmul",
    )(x, w)


def _for_blocks(n, fn):
    def pair(jj, carry):
        fn(2 * jj)
        fn(2 * jj + 1)
        return carry

    lax.fori_loop(0, n // 2, pair, 0)

    @pl.when(n % 2 == 1)
    def _():
        fn(n - 1)


def _lane_cummax(x):
    lane = lax.broadcasted_iota(I32, x.shape, 1)
    s = 1
    while s < x.shape[1]:
        x = jnp.maximum(x, jnp.where(lane >= s, pltpu.roll(x, s, 1), NEG))
        s *= 2
    return x


def _gate_kernel(g_ref, c_ref, r_ref, b_ref, rmax_ref, beta_ref, gc_ref):
    g = g_ref[0]
    c = c_ref[...]
    li = g[0:8] + c[0:8]
    xf = g[8:16] + c[8:16]
    lf = jnp.minimum(xf, 0.0) - jnp.log(1.0 + jnp.exp(-jnp.abs(xf)))
    b = _lane_cumsum(lf, lf.shape[1])
    b_ref[0] = b
    r = li - b
    r_ref[0] = r
    rmax_ref[0] = _lane_cummax(r)
    xa = g[16:24] + c[16:24]
    sp = jnp.maximum(xa, 0.0) + jnp.log(1.0 + jnp.exp(-jnp.abs(xa)))
    gdec = -jnp.exp(c[24:32]) * sp
    gc_ref[0] = _lane_cumsum(gdec, CHUNK)
    beta_ref[0] = _sigmoid(g[24:32])


def _gate_prep(gt, consts):
    bsz, _, t = gt.shape
    out = jax.ShapeDtypeStruct((bsz, 8, t), F32)
    spec = pl.BlockSpec((1, 8, t), lambda b: (b, 0, 0))
    return pl.pallas_call(
        _gate_kernel,
        grid=(bsz,),
        in_specs=[pl.BlockSpec((1, GATE_ROWS, t), lambda b: (b, 0, 0)),
                  pl.BlockSpec((GATE_ROWS, 1), lambda b: (0, 0))],
        out_specs=[spec] * 5,
        out_shape=[out] * 5,
        compiler_params=_params("arbitrary"),
        name="gate_prep",
    )(gt, consts)


def _rotary(t, c, s):
    lane = lax.broadcasted_iota(I32, t.shape, 1)
    swapped = jnp.where((lane % DA_QK_DIM) < DA_QK_DIM // 2,
                        pltpu.roll(t, HEAD - DA_QK_DIM // 2, 1), pltpu.roll(t, DA_QK_DIM // 2, 1))
    return t * c + swapped * s


def _da_kernel(q_ref, k_ref, v_ref, cq_ref, sq_ref, ck_ref, sk_ref, lam_ref, g_ref, o_ref,
               kr_ref, vb_ref, s_ref, mx_ref, ls_ref, acc_ref, *, bq, lambda_init):
    qi = pl.program_id(2)
    nh = bq // HEAD

    @pl.when(qi == 0)
    def _():
        kr_ref[...] = _rotary(k_ref[...], ck_ref[...], sk_ref[...]).astype(BF16)
        vb_ref[...] = v_ref[...].astype(BF16)

    qr = _rotary(q_ref[...], cq_ref[...], sq_ref[...]) * (DA_QK_DIM ** -0.5)
    lane = lax.broadcasted_iota(I32, qr.shape, 1)
    qs = jnp.concatenate([jnp.where(lane < DA_QK_DIM, qr, 0.0),
                          jnp.where(lane >= DA_QK_DIM, qr, 0.0)], axis=0).astype(BF16)

    def fold(x, op):
        out = x[:, :HEAD]
        for i in range(1, nh):
            out = op(out, x[:, i * HEAD:(i + 1) * HEAD])
        return out

    mx_ref[...] = jnp.full(mx_ref.shape, NEG, F32)

    def scores(j):
        off = pl.multiple_of(j * bq, bq)
        s = _dot_nt(qs, kr_ref[pl.ds(off, bq), :])
        s_ref[j] = s
        mx_ref[...] = jnp.maximum(mx_ref[...], fold(s, jnp.maximum))

    _for_blocks(qi, scores)
    offd = pl.multiple_of(qi * bq, bq)
    sd = _dot_nt(qs, kr_ref[pl.ds(offd, bq), :])
    r = lax.broadcasted_iota(I32, sd.shape, 0)
    c = lax.broadcasted_iota(I32, sd.shape, 1)
    sd = jnp.where(jnp.where(r >= bq, r - bq, r) >= c, sd, NEG)
    m = jnp.max(jnp.maximum(mx_ref[...], fold(sd, jnp.maximum)), axis=1, keepdims=True)
    mb = jnp.broadcast_to(m, (2 * bq, HEAD))

    ls_ref[...] = jnp.zeros(ls_ref.shape, F32)
    acc_ref[...] = jnp.zeros(acc_ref.shape, F32)

    def accumulate(s, off):
        ps = [jnp.exp(s[:, i * HEAD:(i + 1) * HEAD] - mb) for i in range(nh)]
        tot = ps[0]
        for p in ps[1:]:
            tot = tot + p
        ls_ref[...] += tot
        pb = jnp.concatenate([p.astype(BF16) for p in ps], axis=1)
        acc_ref[...] += _dot(pb, vb_ref[pl.ds(off, bq), :])

    _for_blocks(qi, lambda j: accumulate(s_ref[j], pl.multiple_of(j * bq, bq)))
    accumulate(sd, offd)

    lam = lam_ref[...]
    s1 = jnp.sum(lam[0:1] * lam[1:2], axis=1, keepdims=True)
    s2 = jnp.sum(lam[2:3] * lam[3:4], axis=1, keepdims=True)
    lam_full = jnp.exp(s1) - jnp.exp(s2) + lambda_init
    acc = acc_ref[...]
    l = jnp.sum(ls_ref[...], axis=1, keepdims=True)
    o = acc[:bq] / l[:bq] - lam_full * (acc[bq:] / l[bq:])
    o = o * lax.rsqrt(jnp.mean(o * o, axis=1, keepdims=True) + 1e-6) * g_ref[...]
    o_ref[...] = (o * (1.0 - lambda_init)).astype(o_ref.dtype)


def _diff_attention(proj, cos_t, sin_t, lam, norm_g, lambda_init, bsz, t):
    bq = ATT_BLOCK
    nq = t // bq
    kern = functools.partial(_da_kernel, bq=bq, lambda_init=lambda_init)
    tab_q = pl.BlockSpec((bq, HEAD), lambda b, h, i: (i, 0))
    tab_k = pl.BlockSpec((t, HEAD), lambda b, h, i: (0, 0))
    return pl.pallas_call(
        kern,
        grid=(bsz, DA_HEADS, nq),
        in_specs=[pl.BlockSpec((bq, HEAD), lambda b, h, i: (b * nq + i, CB_DA_Q + h)),
                  pl.BlockSpec((t, HEAD), lambda b, h, i: (b, CB_DA_K + h)),
                  pl.BlockSpec((t, HEAD), lambda b, h, i: (b, CB_DA_V + h)),
                  tab_q, tab_q, tab_k, tab_k,
                  pl.BlockSpec((4, DA_QK_DIM), lambda b, h, i: (0, 0)),
                  pl.BlockSpec((1, HEAD), lambda b, h, i: (0, 0))],
        out_specs=pl.BlockSpec((bq, HEAD), lambda b, h, i: (b * nq + i, h)),
        out_shape=jax.ShapeDtypeStruct((bsz * t, DA_HEADS * HEAD), BF16),
        scratch_shapes=[pltpu.VMEM((t, HEAD), BF16), pltpu.VMEM((t, HEAD), BF16),
                        pltpu.VMEM((nq, 2 * bq, bq), F32),
                        pltpu.VMEM((2 * bq, HEAD), F32), pltpu.VMEM((2 * bq, HEAD), F32),
                        pltpu.VMEM((2 * bq, HEAD), F32)],
        compiler_params=_params("arbitrary", "arbitrary", "arbitrary"),
        name="diff_attention",
    )(proj, proj, proj, cos_t, sin_t, cos_t, sin_t, lam, norm_g)


def _ml_kernel(q_ref, k_ref, v_ref, op_ref, b_ref, r_ref, rmax_ref, g_ref, o_ref,
               kb_ref, vb_ref, den_ref, num_ref, *, bq):
    qi = pl.program_id(2)
    nh = bq // HEAD

    @pl.when(qi == 0)
    def _():
        kb_ref[...] = (k_ref[...] * (HEAD ** -0.5)).astype(BF16)
        vb_ref[...] = v_ref[...].astype(BF16)

    qb = q_ref[...].astype(BF16)
    b_col = _col_from_row(b_ref[0, 0, pl.ds(qi, 1), :], bq)
    rmax_col = _col_from_row(rmax_ref[0, 0, pl.ds(qi, 1), :], bq)
    den_ref[...] = jnp.zeros(den_ref.shape, F32)
    num_ref[...] = jnp.zeros(num_ref.shape, F32)

    def block(j, masked):
        off = pl.multiple_of(j * bq, bq)
        s = _dot_nt(qb, kb_ref[pl.ds(off, bq), :])
        logw = r_ref[0, 0, pl.ds(j, 1), :] - rmax_col
        if masked:
            r = lax.broadcasted_iota(I32, s.shape, 0)
            c = lax.broadcasted_iota(I32, s.shape, 1)
            logw = jnp.where(r >= c, logw, NEG)
        sw = s * jnp.exp(logw)
        tot = sw[:, :HEAD]
        for i in range(1, nh):
            tot = tot + sw[:, i * HEAD:(i + 1) * HEAD]
        den_ref[...] += tot
        num_ref[...] += _dot(sw.astype(BF16), vb_ref[pl.ds(off, bq), :])

    _for_blocks(qi, lambda j: block(j, False))
    block(qi, True)

    den = jnp.sum(den_ref[...], axis=1, keepdims=True)
    h = num_ref[...] / jnp.maximum(jnp.abs(den), jnp.exp(-(b_col + rmax_col)))
    h = h * lax.rsqrt(jnp.mean(h * h, axis=1, keepdims=True) + 1e-6) * g_ref[...]
    o_ref[...] = (_sigmoid(op_ref[...]) * h).astype(o_ref.dtype)


def _mlstm(proj, b_cum, r_gate, r_max, norm_g, bsz, t):
    bq = ATT_BLOCK
    nq = t // bq
    kern = functools.partial(_ml_kernel, bq=bq)
    gate = pl.BlockSpec((1, 1, nq, bq), lambda b, h, i: (b, h, 0, 0))
    return pl.pallas_call(
        kern,
        grid=(bsz, ML_HEADS, nq),
        in_specs=[pl.BlockSpec((bq, HEAD), lambda b, h, i: (b * nq + i, CB_ML_Q + h)),
                  pl.BlockSpec((t, HEAD), lambda b, h, i: (b, CB_ML_K + h)),
                  pl.BlockSpec((t, HEAD), lambda b, h, i: (b, CB_ML_V + h)),
                  pl.BlockSpec((bq, HEAD), lambda b, h, i: (b * nq + i, CB_ML_O + h)),
                  gate, gate, gate,
                  pl.BlockSpec((1, HEAD), lambda b, h, i: (0, 0))],
        out_specs=pl.BlockSpec((bq, HEAD), lambda b, h, i: (b * nq + i, h)),
        out_shape=jax.ShapeDtypeStruct((bsz * t, ML_HEADS * HEAD), BF16),
        scratch_shapes=[pltpu.VMEM((t, HEAD), BF16), pltpu.VMEM((t, HEAD), BF16),
                        pltpu.VMEM((bq, HEAD), F32), pltpu.VMEM((bq, HEAD), F32)],
        compiler_params=_params("arbitrary", "arbitrary", "arbitrary"),
        name="mlstm",
    )(proj, proj, proj, proj, b_cum.reshape(bsz, 8, nq, bq), r_gate.reshape(bsz, 8, nq, bq),
      r_max.reshape(bsz, 8, nq, bq), norm_g)


def _gd_kernel(q_ref, k_ref, v_ref, z_ref, cwq_ref, cwk_ref, cwv_ref, beta_ref, gc_ref, g_ref, o_ref,
               mq_ref, no_ref, egl_ref, st_a, st_attn, st_rhs, st_kd, st_qg, *, t, heads):
    L = CHUNK
    S2 = GD_PACK * L
    nc = t // L
    ri = lax.broadcasted_iota(I32, (S2, S2), 0)
    ci = lax.broadcasted_iota(I32, (S2, S2), 1)
    same = (ri // L) == (ci // L)
    tri = same & (ri >= ci)
    strict = same & (ri > ci)
    eye = (ri == ci).astype(F32)
    row_chunk = lax.broadcasted_iota(I32, (S2, 1), 0) // L

    def conv_silu(x_ref, cw_ref, lo, sc, base):
        cur = x_ref[pl.ds(base, S2), lo:lo + HEAD]
        prev = x_ref[pl.ds(pl.multiple_of(jnp.maximum(base - 8, 0), 8), 8), lo:lo + HEAD]
        blk = jnp.concatenate([jnp.where(sc > 0, prev, 0.0), cur], axis=0)
        cw = cw_ref[:, lo:lo + HEAD]
        y = blk[5:5 + S2] * cw[0:1]
        for i in range(1, CONV_W):
            y = y + blk[5 + i:5 + i + S2] * cw[i:i + 1]
        return y * _sigmoid(y)

    def mm(a, b):
        return _dot(a.astype(BF16), b.astype(BF16))

    def prepare(hh, sc, slot):
        lo = hh * HEAD
        base = pl.multiple_of(sc * S2, S2)
        q = conv_silu(q_ref, cwq_ref, lo, sc, base)
        k = conv_silu(k_ref, cwk_ref, lo, sc, base)
        v = conv_silu(v_ref, cwv_ref, lo, sc, base)
        q = q * lax.rsqrt(jnp.sum(q * q, axis=1, keepdims=True) + 1e-6) * (HEAD ** -0.5)
        k = k * lax.rsqrt(jnp.sum(k * k, axis=1, keepdims=True) + 1e-6)
        gc_row = gc_ref[0, hh, pl.ds(sc, 1), :]
        gc_col = _col_from_row(gc_row, S2)
        beta_col = _col_from_row(beta_ref[0, hh, pl.ds(sc, 1), :], S2)
        decay = jnp.exp(jnp.where(tri, gc_col - gc_row, NEG))
        kb = k * beta_col
        kq = _dot_nt(jnp.concatenate([kb, q], axis=0).astype(BF16), k.astype(BF16))
        st_a[slot, hh] = jnp.where(strict, kq[:S2] * decay, 0.0)
        st_attn[slot, hh] = jnp.where(tri, kq[S2:] * decay, 0.0).astype(BF16)
        st_rhs[slot, hh] = jnp.concatenate([kb * jnp.exp(gc_col), v * beta_col], axis=1).astype(BF16)
        gls = [gc_row[:, c * L + L - 1:c * L + L] for c in range(GD_PACK)]
        gl_col = gls[-1]
        for c in reversed(range(GD_PACK - 1)):
            gl_col = jnp.where(row_chunk == c, gls[c], gl_col)
        st_kd[slot, hh] = (k * jnp.exp(gl_col - gc_col)).astype(BF16)
        st_qg[slot, hh] = q * jnp.exp(gc_col)
        for c in range(GD_PACK):
            egl_ref[hh, pl.ds(GD_PACK * sc + c, 1), :] = jnp.broadcast_to(jnp.exp(gls[c]), (1, HEAD))

    def solve(hh, sc, slot):
        a = st_a[slot, hh]
        attn = st_attn[slot, hh]
        x = eye - a
        p = mm(a, a)
        for _ in range(4):
            px = mm(jnp.concatenate([p, x], axis=0), p)
            p = px[:S2]
            x = x + px[S2:]
        x = x + mm(x, p)
        wu = _dot(x.astype(BF16), st_rhs[slot, hh])
        kd = st_kd[slot, hh]
        aw = _dot(attn, wu.astype(BF16))
        qprime = (st_qg[slot, hh] - aw[:, :HEAD]).astype(BF16)
        wbd = jnp.concatenate([jnp.where(row_chunk == c, wu, 0.0).astype(BF16) for c in range(GD_PACK)],
                              axis=1)
        mn = _dot_tn(kd, wbd)
        for c in range(GD_PACK):
            rows = slice(c * L, (c + 1) * L)
            blk = pl.multiple_of((GD_PACK * sc + c) * (HEAD + L), L)
            mq_ref[hh, pl.ds(blk, HEAD), :] = mn[:, 2 * c * HEAD:(2 * c + 1) * HEAD].astype(BF16)
            mq_ref[hh, pl.ds(blk + HEAD, L), :] = qprime[rows]
            no_ref[hh, pl.ds(blk, HEAD), :] = mn[:, (2 * c + 1) * HEAD:(2 * c + 2) * HEAD]
            no_ref[hh, pl.ds(blk + HEAD, L), :] = aw[rows, HEAD:]

    nsc = nc // GD_PACK
    assert nsc % 2 == 0
    def first(sc, carry):
        for hh in range(heads):
            prepare(hh, sc, 0)
        return carry

    lax.fori_loop(0, 1, first, 0)

    def local(j, carry):
        for slot in range(2):
            sc = 2 * j + slot
            for hh in range(heads):
                solve(hh, sc, slot)
            for hh in range(heads):
                prepare(hh, jnp.minimum(sc + 1, nsc - 1), 1 - slot)
        return carry

    lax.fori_loop(0, nsc // 2, local, 0)

    gn = g_ref[...]

    def step(c, states):
        base = pl.multiple_of(c * L, L)
        blk = pl.multiple_of(c * (HEAD + L), L)
        new = []
        for hh in range(heads):
            lo = hh * HEAD
            s = states[hh]
            ms = _dot(mq_ref[hh, pl.ds(blk, HEAD + L), :], s.astype(BF16))
            new.append(s * egl_ref[hh, pl.ds(c, 1), :] - ms[:HEAD] + no_ref[hh, pl.ds(blk, HEAD), :])
            o = ms[HEAD:] + no_ref[hh, pl.ds(blk + HEAD, L), :]
            o = o * lax.rsqrt(jnp.mean(o * o, axis=1, keepdims=True) + 1e-6) * gn
            z = z_ref[pl.ds(base, L), lo:lo + HEAD]
            o_ref[pl.ds(base, L), lo:lo + HEAD] = (o * (z * _sigmoid(z))).astype(o_ref.dtype)
        return tuple(new)

    lax.fori_loop(0, nc, step, tuple(jnp.zeros((HEAD, HEAD), F32) for _ in range(heads)), unroll=2)


def _gated_deltanet(proj, conv_w, beta, gc, norm_g, bsz, t, heads=2):
    nc = t // CHUNK
    width = heads * HEAD
    kern = functools.partial(_gd_kernel, t=t, heads=heads)

    def slab(cb):
        return pl.BlockSpec((t, width), lambda b, h: (b, cb // heads + h))

    def cw(off):
        return pl.BlockSpec((CONV_W, width), lambda b, h: (0, off // heads + h))

    nsc, srows = nc // GD_PACK, GD_PACK * CHUNK
    gate = pl.BlockSpec((1, heads, nsc, srows), lambda b, h: (b, h, 0, 0))
    return pl.pallas_call(
        kern,
        grid=(bsz, GD_HEADS // heads),
        in_specs=[slab(CB_GD_Q), slab(CB_GD_K), slab(CB_GD_V), slab(CB_GD_Z),
                  cw(0), cw(GD_HEADS), cw(2 * GD_HEADS), gate, gate,
                  pl.BlockSpec((1, HEAD), lambda b, h: (0, 0))],
        out_specs=pl.BlockSpec((t, width), lambda b, h: (b, h)),
        out_shape=jax.ShapeDtypeStruct((bsz * t, GD_HEADS * HEAD), BF16),
        scratch_shapes=[pltpu.VMEM((heads, nc * (HEAD + CHUNK), HEAD), BF16),
                        pltpu.VMEM((heads, nc * (HEAD + CHUNK), HEAD), F32),
                        pltpu.VMEM((heads, nc, HEAD), F32),
                        pltpu.VMEM((2, heads, srows, srows), F32),
                        pltpu.VMEM((2, heads, srows, srows), BF16),
                        pltpu.VMEM((2, heads, srows, 2 * HEAD), BF16),
                        pltpu.VMEM((2, heads, srows, HEAD), BF16),
                        pltpu.VMEM((2, heads, srows, HEAD), F32)],
        compiler_params=_params("arbitrary", "arbitrary"),
        name="gated_deltanet",
    )(proj, proj, proj, proj, conv_w, conv_w, conv_w,
      beta.reshape(bsz, 8, nsc, srows), gc.reshape(bsz, 8, nsc, srows), norm_g)


def _layer_norm_rows(z, g, b):
    mu = jnp.mean(z, axis=1, keepdims=True)
    zc = z - mu
    var = jnp.mean(zc * zc, axis=1, keepdims=True)
    return zc * lax.rsqrt(var + 1e-5) * g + b


def _proj_ln_kernel(*refs, nx, alpha, tn):
    x_refs = refs[:nx]
    w_refs = refs[nx:2 * nx]
    h_ref, g_ref, b_ref, o_ref, z_ref = refs[2 * nx:]
    for n in range(0, D_MODEL, tn):
        acc = alpha * h_ref[:, n:n + tn]
        for x_ref, w_ref in zip(x_refs, w_refs):
            acc = acc + _dot(x_ref[...], w_ref[:, n:n + tn])
        z_ref[:, n:n + tn] = acc
    o_ref[...] = _layer_norm_rows(z_ref[...], g_ref[...], b_ref[...])


def _proj_ln(xs, w, h, g, b, alpha, tm=512):
    n_rows = h.shape[0]
    nx = len(xs)
    kern = functools.partial(_proj_ln_kernel, nx=nx, alpha=alpha, tn=512)
    row = lambda i: (i, 0)
    fixed = lambda i: (0, 0)
    w_specs, start = [], 0
    for x in xs:
        width = x.shape[1]
        w_specs.append(pl.BlockSpec((width, D_MODEL), functools.partial(lambda i, blk: (blk, 0),
                                                                         blk=start // width)))
        start += width
    in_specs = ([pl.BlockSpec((tm, x.shape[1]), row) for x in xs]
                + w_specs
                + [pl.BlockSpec((tm, D_MODEL), row),
                   pl.BlockSpec((1, D_MODEL), fixed), pl.BlockSpec((1, D_MODEL), fixed)])
    return pl.pallas_call(
        kern,
        grid=(n_rows // tm,),
        in_specs=in_specs,
        out_specs=pl.BlockSpec((tm, D_MODEL), row),
        out_shape=jax.ShapeDtypeStruct((n_rows, D_MODEL), F32),
        scratch_shapes=[pltpu.VMEM((tm, D_MODEL), F32)],
        compiler_params=_params("arbitrary"),
        name="proj_ln",
    )(*xs, *([w] * nx), h, g, b)


def _xa_kernel(q_ref, kv_ref, o_ref):
    for hh in range(XA_HEADS):
        lo = hh * XA_DIM
        s = _dot_nt(q_ref[:, lo:lo + XA_DIM], kv_ref[:, lo:lo + XA_DIM]) * (XA_DIM ** -0.5)
        s = s - jnp.max(s, axis=1, keepdims=True)
        p = jnp.exp(s)
        p = p / jnp.sum(p, axis=1, keepdims=True)
        o = _dot(p.astype(BF16), kv_ref[:, D_MODEL + lo:D_MODEL + lo + XA_DIM])
        o_ref[:, lo:lo + XA_DIM] = o.astype(o_ref.dtype)


def _cross_attention(q, kv, bsz, t, mem_len, tq=512):
    nq = t // tq
    return pl.pallas_call(
        _xa_kernel,
        grid=(bsz, nq),
        in_specs=[pl.BlockSpec((tq, D_MODEL), lambda b, i: (b * nq + i, 0)),
                  pl.BlockSpec((mem_len, 2 * D_MODEL), lambda b, i: (b, 0))],
        out_specs=pl.BlockSpec((tq, D_MODEL), lambda b, i: (b * nq + i, 0)),
        out_shape=jax.ShapeDtypeStruct((bsz * t, D_MODEL), BF16),
        compiler_params=_params("arbitrary", "arbitrary"),
        name="cross_attention",
    )(q, kv)


def _router_kernel(x_ref, rw_ref, bias_ref, ids_ref, wts_ref, cnt_ref, carry_ref, *, tm):
    i = pl.program_id(0)

    @pl.when(i == 0)
    def _():
        carry_ref[...] = jnp.zeros(carry_ref.shape, F32)

    logits = _dot_nt(rw_ref[...], x_ref[...].astype(BF16))
    scores = _sigmoid(logits)
    sel = scores + bias_ref[...]
    srow = [sel[e:e + 1] for e in range(N_EXPERTS)]
    prow = [scores[e:e + 1] for e in range(N_EXPERTS)]

    gscore = []
    for g in range(N_GROUPS):
        s0, s1, s2, s3 = srow[4 * g:4 * g + 4]
        a, b = jnp.maximum(s0, s1), jnp.minimum(s0, s1)
        c, d = jnp.maximum(s2, s3), jnp.minimum(s2, s3)
        gscore.append(jnp.maximum(a, c) + jnp.maximum(jnp.minimum(a, c), jnp.maximum(b, d)))
    gmax = jnp.maximum(jnp.maximum(gscore[0], gscore[1]), jnp.maximum(gscore[2], gscore[3]))
    gi = jnp.where(gscore[0] == gmax, 0, jnp.where(gscore[1] == gmax, 1,
                                                    jnp.where(gscore[2] == gmax, 2, 3)))

    def pick(rows, j):
        return jnp.where(gi == 0, rows[j], jnp.where(gi == 1, rows[4 + j],
                                                      jnp.where(gi == 2, rows[8 + j], rows[12 + j])))

    v = [pick(srow, j) for j in range(GROUP_SIZE)]
    sc = [pick(prow, j) for j in range(GROUP_SIZE)]
    v1 = jnp.maximum(jnp.maximum(v[0], v[1]), jnp.maximum(v[2], v[3]))
    i1 = jnp.where(v[0] == v1, 0, jnp.where(v[1] == v1, 1, jnp.where(v[2] == v1, 2, 3)))
    rest = [jnp.where(i1 == j, -jnp.inf, v[j]) for j in range(GROUP_SIZE)]
    v2 = jnp.maximum(jnp.maximum(rest[0], rest[1]), jnp.maximum(rest[2], rest[3]))
    i2 = jnp.where(rest[0] == v2, 0, jnp.where(rest[1] == v2, 1, jnp.where(rest[2] == v2, 2, 3)))

    def pick_idx(idx):
        return jnp.where(idx == 0, sc[0], jnp.where(idx == 1, sc[1], jnp.where(idx == 2, sc[2], sc[3])))

    p1, p2 = pick_idx(i1), pick_idx(i2)
    tot = p1 + p2
    e1 = 4 * gi + i1
    e2 = 4 * gi + i2

    erow = lax.broadcasted_iota(I32, (N_EXPERTS, tm), 0)
    hit1 = erow == e1
    hit2 = erow == e2
    onehot = jnp.where(hit1 | hit2, 1.0, 0.0)
    incl = _lane_cumsum(onehot, tm)
    excl = incl - onehot + carry_ref[...]
    rank1 = jnp.sum(jnp.where(hit1, excl, 0.0), axis=0, keepdims=True)
    rank2 = jnp.sum(jnp.where(hit2, excl, 0.0), axis=0, keepdims=True)
    carry_ref[...] = carry_ref[...] + incl[:, tm - 1:tm]

    zi = jnp.zeros((4, tm), I32)
    ids_ref[...] = jnp.concatenate([e1, e2, rank1.astype(I32), rank2.astype(I32), zi], axis=0)
    wts_ref[...] = jnp.concatenate([p1 / tot, p2 / tot, jnp.zeros((6, tm), F32)], axis=0)
    cnt_ref[...] = jnp.broadcast_to(carry_ref[...], cnt_ref.shape)


def _router(h, rw_t, bias_col, tm=2048):
    n = h.shape[0]
    tm = min(tm, n)
    kern = functools.partial(_router_kernel, tm=tm)
    return pl.pallas_call(
        kern,
        grid=(n // tm,),
        in_specs=[pl.BlockSpec((tm, D_MODEL), lambda i: (i, 0)),
                  pl.BlockSpec((N_EXPERTS, D_MODEL), lambda i: (0, 0)),
                  pl.BlockSpec((N_EXPERTS, 1), lambda i: (0, 0))],
        out_specs=[pl.BlockSpec((8, tm), lambda i: (0, i)),
                   pl.BlockSpec((8, tm), lambda i: (0, i)),
                   pl.BlockSpec((N_EXPERTS, HEAD), lambda i: (0, 0))],
        out_shape=[jax.ShapeDtypeStruct((8, n), I32), jax.ShapeDtypeStruct((8, n), F32),
                   jax.ShapeDtypeStruct((N_EXPERTS, HEAD), F32)],
        scratch_shapes=[pltpu.VMEM((N_EXPERTS, 1), F32)],
        compiler_params=_params("arbitrary"),
        name="router",
    )(h, rw_t, bias_col)


def _row_copy(src_hbm, row, dst_ref, slot, sem):
    return pltpu.make_async_copy(src_hbm.at[pl.ds(row, 1), :], dst_ref.at[pl.ds(slot, 1), :], sem)


def _expert_kernel(texp_ref, nused_ref, src_ref, h_hbm, w1_ref, w2_ref, o_ref, *scratch, tm):
    i = pl.program_id(0)
    n_used = nused_ref[0]
    bufs, sem = scratch[:-1], scratch[-1]
    depth = len(bufs)

    def wait_rows(buf):
        pltpu.make_async_copy(h_hbm.at[pl.ds(0, tm), :], bufs[buf], sem.at[buf]).wait()

    @pl.when(i == 0)
    def _():
        for first in range(depth - 1):
            def body(r, carry, first=first):
                _row_copy(h_hbm, src_ref[first * tm + r], bufs[first], r, sem.at[first]).start()
                return carry

            lax.fori_loop(0, tm, body, 0, unroll=8)

    def tile(cur):
        wait_rows(cur)
        ahead = (cur + depth - 1) % depth
        nxt = jnp.minimum(i + depth - 1, n_used - 1) * tm
        for r in range(tm):
            _row_copy(h_hbm, src_ref[nxt + r], bufs[ahead], r, sem.at[ahead]).start(priority=r % 2)
        mid = _dot(bufs[cur][...].astype(BF16), w1_ref[0])
        gate = mid[:, :D_FF]
        act = (gate * _sigmoid(gate)) * mid[:, D_FF:]
        o_ref[...] = _dot(act.astype(BF16), w2_ref[0])

        @pl.when(i + 1 == n_used)
        def _():
            for k in range(1, depth):
                wait_rows((cur + k) % depth)

    for cur in range(depth):
        @pl.when((i < n_used) & (i % depth == cur))
        def _(cur=cur):
            tile(cur)

    @pl.when(i >= n_used)
    def _():
        o_ref[...] = jnp.zeros(o_ref.shape, F32)


def _experts(h, w1, w2, tile_expert, n_used, src_rows, n_tiles):
    tm = MOE_TILE
    kern = functools.partial(_expert_kernel, tm=tm)
    grid_spec = pltpu.PrefetchScalarGridSpec(
        num_scalar_prefetch=3,
        grid=(n_tiles,),
        in_specs=[pl.BlockSpec(memory_space=pl.ANY),
                  pl.BlockSpec((1, D_MODEL, 2 * D_FF), lambda i, te, nu, sr: (te[i], 0, 0)),
                  pl.BlockSpec((1, D_FF, D_MODEL), lambda i, te, nu, sr: (te[i], 0, 0))],
        out_specs=pl.BlockSpec((tm, D_MODEL), lambda i, te, nu, sr: (i, 0)),
        scratch_shapes=([pltpu.VMEM((tm, D_MODEL), F32)] * MOE_ROW_BUFFERS
                        + [pltpu.SemaphoreType.DMA((MOE_ROW_BUFFERS,))]),
    )
    return pl.pallas_call(
        kern,
        grid_spec=grid_spec,
        out_shape=jax.ShapeDtypeStruct((n_tiles * tm, D_MODEL), F32),
        compiler_params=_params("arbitrary"),
        name="experts",
    )(tile_expert, n_used, src_rows, h, w1, w2)


def _combine_kernel(p1_ref, p2_ref, y_hbm, h_ref, w1_ref, w2_ref, g_ref, b_ref, o_ref,
                    a0_ref, a1_ref, a2_ref, c0_ref, c1_ref, c2_ref, sem_a, sem_c, *, tc, alpha):
    i = pl.program_id(0)
    last = pl.num_programs(0) - 1
    a_bufs = (a0_ref, a1_ref, a2_ref)
    c_bufs = (c0_ref, c1_ref, c2_ref)
    depth = len(a_bufs)

    def wait_rows(buf):
        pltpu.make_async_copy(y_hbm.at[pl.ds(0, tc), :], a_bufs[buf], sem_a.at[buf]).wait()
        pltpu.make_async_copy(y_hbm.at[pl.ds(0, tc), :], c_bufs[buf], sem_c.at[buf]).wait()

    @pl.when(i == 0)
    def _():
        for first in range(depth - 1):
            def body(r, carry, first=first):
                _row_copy(y_hbm, p1_ref[first * tc + r], a_bufs[first], r, sem_a.at[first]).start()
                _row_copy(y_hbm, p2_ref[first * tc + r], c_bufs[first], r, sem_c.at[first]).start()
                return carry

            lax.fori_loop(0, tc, body, 0, unroll=8)

    def step(cur):
        wait_rows(cur)
        ahead = (cur + depth - 1) % depth
        nxt = jnp.minimum(i + depth - 1, last) * tc
        for r in range(tc):
            _row_copy(y_hbm, p1_ref[nxt + r], a_bufs[ahead], r, sem_a.at[ahead]).start(priority=0)
            _row_copy(y_hbm, p2_ref[nxt + r], c_bufs[ahead], r, sem_c.at[ahead]).start(priority=1)
        z = alpha * h_ref[...] + (w1_ref[...] * a_bufs[cur][...] + w2_ref[...] * c_bufs[cur][...])
        o_ref[...] = _layer_norm_rows(z, g_ref[...], b_ref[...])

        @pl.when(i == last)
        def _():
            for k in range(1, depth):
                wait_rows((cur + k) % depth)

    for cur in range(depth):
        @pl.when(i % depth == cur)
        def _(cur=cur):
            step(cur)


def _combine(y, h, pos1, pos2, w1c, w2c, g, b, alpha, tc=256):
    n = h.shape[0]
    kern = functools.partial(_combine_kernel, tc=tc, alpha=alpha)
    row = lambda i, p1, p2: (i, 0)
    fixed = lambda i, p1, p2: (0, 0)
    grid_spec = pltpu.PrefetchScalarGridSpec(
        num_scalar_prefetch=2,
        grid=(n // tc,),
        in_specs=[pl.BlockSpec(memory_space=pl.ANY),
                  pl.BlockSpec((tc, D_MODEL), row),
                  pl.BlockSpec((tc, 1), row), pl.BlockSpec((tc, 1), row),
                  pl.BlockSpec((1, D_MODEL), fixed), pl.BlockSpec((1, D_MODEL), fixed)],
        out_specs=pl.BlockSpec((tc, D_MODEL), row),
        scratch_shapes=[pltpu.VMEM((tc, D_MODEL), F32)] * 6
                       + [pltpu.SemaphoreType.DMA((3,)), pltpu.SemaphoreType.DMA((3,))],
    )
    return pl.pallas_call(
        kern,
        grid_spec=grid_spec,
        out_shape=jax.ShapeDtypeStruct((n, D_MODEL), F32),
        compiler_params=_params("arbitrary"),
        name="moe_combine",
    )(pos1, pos2, y, h, w1c, w2c, g, b)


def _src_rows_kernel(p1_ref, p2_ref, lo_ref, hi_ref, o_ref):
    def clear_range(e, carry):
        def clear(r, c):
            o_ref[r] = 0
            return c

        lax.fori_loop(lo_ref[e], hi_ref[e], clear, 0)
        return carry

    lax.fori_loop(0, lo_ref.shape[0], clear_range, 0)

    def place(tok, carry):
        o_ref[p1_ref[tok]] = tok
        o_ref[p2_ref[tok]] = tok
        return carry

    lax.fori_loop(0, p1_ref.shape[0], place, 0, unroll=8)


def _src_rows(pos1, pos2, pad_lo, pad_hi, total):
    smem = pl.BlockSpec(memory_space=pltpu.SMEM)
    return pl.pallas_call(
        _src_rows_kernel,
        in_specs=[smem, smem, smem, smem],
        out_specs=smem,
        out_shape=jax.ShapeDtypeStruct((total,), I32),
        name="expert_row_sources",
    )(pos1, pos2, pad_lo, pad_hi)


def _moe(h, rw_t, bias_col, w1, w2, g, b, alpha):
    n = h.shape[0]
    tm = MOE_TILE
    n_tiles = (2 * n) // tm + N_EXPERTS
    ids, wts, cnt = _router(h, rw_t, bias_col)
    counts = cnt[:, 0].astype(I32)
    padded = ((counts + tm - 1) // tm) * tm
    ends = jnp.cumsum(padded)
    offs = ends - padded
    pos1 = offs[ids[0]] + ids[2]
    pos2 = offs[ids[1]] + ids[3]
    total = n_tiles * tm
    pad_lo = jnp.concatenate([offs + counts, ends[-1:]]).astype(I32)
    pad_hi = jnp.concatenate([ends, jnp.full((1,), total, I32)]).astype(I32)
    src_rows = _src_rows(pos1, pos2, pad_lo, pad_hi, total)
    tile_start = jnp.arange(n_tiles, dtype=I32) * tm
    tile_expert = jnp.minimum(jnp.sum(tile_start[:, None] >= ends[None, :], axis=1),
                              N_EXPERTS - 1).astype(I32)
    n_used = (ends[-1:] // tm).astype(I32)
    y = _experts(h, w1, w2, tile_expert, n_used, src_rows, n_tiles)
    return _combine(y, h, pos1, pos2, wts[0].reshape(n, 1), wts[1].reshape(n, 1), g, b, alpha)


def _rotary_tables(t):
    inv = 1.0 / (ROPE_THETA ** (jnp.arange(0, DA_QK_DIM, 2, dtype=F32) / DA_QK_DIM))
    ang = jnp.arange(t, dtype=F32)[:, None] * inv[None, :]
    cos, sin = jnp.cos(ang), jnp.sin(ang)
    return jnp.tile(cos, (1, 4)), jnp.concatenate([-sin, sin, -sin, sin], axis=1)


def _split_w_in(w):
    main = jnp.concatenate([w[:, :3584], w[:, 3592:7688]], axis=1).astype(BF16)
    zero4 = jnp.zeros((w.shape[0], 4), w.dtype)
    gate = jnp.concatenate([w[:, 3584:3588], zero4, w[:, 3588:3592], zero4, w[:, 7688:7704],
                            jnp.zeros((w.shape[0], HEAD - GATE_ROWS), w.dtype)], axis=1).astype(BF16)
    return main, gate


def kernel(x, mem, w_in, da_lambda, da_norm_g, ml_gate_b, ml_norm_g, gd_conv_w, gd_a_log, gd_dt_bias,
           gd_norm_g, w_out, xa_wq, xa_wkv, xa_wo, router_w, router_bias, moe_w_in, moe_w_out,
           ln_g, ln_b):
    bsz, t, d = x.shape
    mem_len = mem.shape[1]
    depth = w_in.shape[0]
    n = bsz * t
    alpha = (2 * depth) ** 0.25
    cos_t, sin_t = _rotary_tables(t)
    rw_t = router_w.T.astype(BF16)
    bias_col = router_bias.reshape(N_EXPERTS, 1).astype(F32)
    mem2 = mem.reshape(bsz * mem_len, d)
    zero4 = jnp.zeros((4,), F32)
    h = x.reshape(n, d)
    for l in range(depth):
        lambda_init = 0.8 - 0.6 * math.exp(-0.3 * l)
        w_main, w_gate = _split_w_in(w_in[l])
        proj = _matmul(h, w_main, F32, 1024, 512)
        gates = _matmul(h, w_gate, F32, 1024, HEAD)
        gt = gates[:, :GATE_ROWS].reshape(bsz, t, GATE_ROWS).transpose(0, 2, 1)
        consts = jnp.concatenate([ml_gate_b[l, 0], zero4, ml_gate_b[l, 1], zero4,
                                  gd_dt_bias[l], gd_a_log[l]]).reshape(GATE_ROWS, 1)
        r_gate, b_cum, r_max, beta, gc = _gate_prep(gt, consts)
        y_da = _diff_attention(proj, cos_t, sin_t, da_lambda[l], da_norm_g[l].reshape(1, HEAD),
                               lambda_init, bsz, t)
        y_ml = _mlstm(proj, b_cum, r_gate, r_max, ml_norm_g[l].reshape(1, HEAD), bsz, t)
        y_gd = _gated_deltanet(proj, gd_conv_w[l], beta, gc, gd_norm_g[l].reshape(1, HEAD), bsz, t)
        h = _proj_ln([y_da, y_ml, y_gd], w_out[l].astype(BF16), h,
                     ln_g[l, 0].reshape(1, d), ln_b[l, 0].reshape(1, d), alpha)
        q = _matmul(h, xa_wq[l].astype(BF16), BF16, 1024, 512)
        kv = _matmul(mem2, xa_wkv[l].astype(BF16), BF16, 1024, 512)
        xo = _cross_attention(q, kv, bsz, t, mem_len)
        h = _proj_ln([xo], xa_wo[l].astype(BF16), h,
                     ln_g[l, 1].reshape(1, d), ln_b[l, 1].reshape(1, d), alpha)
        h = _moe(h, rw_t, bias_col, moe_w_in[l].astype(BF16), moe_w_out[l].astype(BF16),
                 ln_g[l, 2].reshape(1, d), ln_b[l, 2].reshape(1, d), alpha)
    return h.reshape(bsz, t, d)
```
